```python
import math
import jax, jax.numpy as jnp
from jax import lax
import numpy as np

D_MODEL = 1024
BATCH = 8
SEQ = 8192
DEPTH = 1
DEC_BATCH = 16
DEC_SEQ = 64
PAST_LEN = 2048

CHUNK = 64
HEAD_DIM = D_MODEL // 16
N_HEADS_SB = (D_MODEL // 2) // HEAD_DIM
N_HEADS_BAND = (D_MODEL // 2) // HEAD_DIM
W_SB = N_HEADS_SB * HEAD_DIM
W_BAND = N_HEADS_BAND * HEAD_DIM
MIX_WIDTH = W_SB + W_BAND
D_FF = 4 * D_MODEL
PAST_CHUNKS = 8
BAND_KEYS = (PAST_CHUNKS + 1) * CHUNK
REL_CLIP = 2 * CHUNK
SB_BLOCK = 128
EPS = 1e-6
NEG_INF = -1e30

kernel_name = "hymba_stickbreak_chunkband_stream"


def rmsnorm(x, g):
    xf = x.astype(jnp.float32)
    y = xf * lax.rsqrt(jnp.mean(xf * xf, axis=-1, keepdims=True) + EPS)
    return (y * g.astype(jnp.float32)).astype(x.dtype)


def project(xn, w_in):
    p = xn @ w_in
    cuts = [W_SB, 2 * W_SB, 3 * W_SB, 3 * W_SB + W_BAND, 3 * W_SB + 2 * W_BAND]
    q_sb, k_sb, v_sb, q_bd, k_bd, v_bd = jnp.split(p, cuts, axis=-1)
    B, T = xn.shape[0], xn.shape[1]
    sb = [a.reshape(B, T, N_HEADS_SB, HEAD_DIM) for a in (q_sb, k_sb, v_sb)]
    bd = [a.reshape(B, T, N_HEADS_BAND, HEAD_DIM) for a in (q_bd, k_bd, v_bd)]
    return sb[0], sb[1], sb[2], bd[0], bd[1], bd[2]


def stick_breaking(q, k, v, q_pos):
    z = jnp.einsum('bqhd,bkhd->bhqk', q, k).astype(jnp.float32) * (1.0 / math.sqrt(HEAD_DIM))
    k_pos = jnp.arange(k.shape[1])
    mask = k_pos[None, :] < q_pos[:, None]
    log_rem = jnp.where(mask, jax.nn.log_sigmoid(-z), 0.0)
    after = lax.cumsum(log_rem, axis=3, reverse=True) - log_rem
    w = jnp.where(mask, jnp.exp(jax.nn.log_sigmoid(z) + after), 0.0)
    return jnp.einsum('bhqk,bkhd->bqhd', w.astype(v.dtype), v)


def stick_breaking_prompt(q, k, v):
    B, S, H, d = q.shape
    nb = S // SB_BLOCK
    q_blocks = jnp.moveaxis(q.reshape(B, nb, SB_BLOCK, H, d), 1, 0)

    def one_block(args):
        qb, bi = args
        q_pos = bi * SB_BLOCK + jnp.arange(SB_BLOCK)
        return stick_breaking(qb, k, v, q_pos)

    out = lax.map(one_block, (q_blocks, jnp.arange(nb)))
    return jnp.moveaxis(out, 0, 1).reshape(B, S, H, d)


def band_attend(q, k, v, q_pos, k_pos, rel_bias):
    s = jnp.einsum('bqhd,bkhd->bhqk', q, k).astype(jnp.float32) * (1.0 / math.sqrt(HEAD_DIM))
    rel = jnp.clip(q_pos[:, None] - k_pos[None, :], -REL_CLIP, REL_CLIP) + REL_CLIP
    s = s + rel_bias.astype(jnp.float32)[:, rel][None]
    s = jnp.where((k_pos >= 0)[None, None, None, :], s, NEG_INF)
    p = jax.nn.softmax(s, axis=-1)
    return jnp.einsum('bhqk,bkhd->bqhd', p.astype(v.dtype), v)


def band_prompt(q, k, v, rel_bias):
    B, S, H, d = q.shape
    nc = S // CHUNK
    pad = PAST_CHUNKS * CHUNK
    k_pad = jnp.pad(k, ((0, 0), (pad, 0), (0, 0), (0, 0)))
    v_pad = jnp.pad(v, ((0, 0), (pad, 0), (0, 0), (0, 0)))

    def one_chunk(c):
        qc = lax.dynamic_slice_in_dim(q, c * CHUNK, CHUNK, axis=1)
        kc = lax.dynamic_slice_in_dim(k_pad, c * CHUNK, BAND_KEYS, axis=1)
        vc = lax.dynamic_slice_in_dim(v_pad, c * CHUNK, BAND_KEYS, axis=1)
        q_pos = c * CHUNK + jnp.arange(CHUNK)
        k_pos = (c - PAST_CHUNKS) * CHUNK + jnp.arange(BAND_KEYS)
        return band_attend(qc, kc, vc, q_pos, k_pos, rel_bias)

    out = lax.map(one_chunk, jnp.arange(nc))
    return jnp.moveaxis(out, 0, 1).reshape(B, S, H, d)


def mix_out(o_sb, o_bd, g_sb, g_bd, w_out):
    B, T = o_sb.shape[0], o_sb.shape[1]
    y_sb = rmsnorm(o_sb.reshape(B, T, W_SB), g_sb)
    y_bd = rmsnorm(o_bd.reshape(B, T, W_BAND), g_bd)
    return jnp.concatenate([y_sb, y_bd], axis=-1) @ w_out


def ffn(h, g, w_up, w_down):
    u = rmsnorm(h, g) @ w_up
    return jnp.square(jax.nn.relu(u)) @ w_down


def setup_inputs(seed: int = 0) -> dict:
    key = jax.random.key(seed)
    ks = jax.random.split(key, 16)
    f32 = jnp.float32
    band_rows = min(PAST_CHUNKS * CHUNK, PAST_LEN)
    nrm = lambda k, shape, s: (jax.random.normal(k, shape, f32) * s)
    return {
        "x_prompt": nrm(ks[0], (BATCH, SEQ, D_MODEL), 1.0),
        "x_sample": nrm(ks[1], (DEC_BATCH, DEC_SEQ, D_MODEL), 1.0),
        "cache_sb_k": nrm(ks[2], (DEPTH, DEC_BATCH, PAST_LEN, N_HEADS_SB, HEAD_DIM), 1.0),
        "cache_sb_v": nrm(ks[3], (DEPTH, DEC_BATCH, PAST_LEN, N_HEADS_SB, HEAD_DIM), 1.0),
        "cache_band_k": nrm(ks[4], (DEPTH, DEC_BATCH, band_rows, N_HEADS_BAND, HEAD_DIM), 1.0),
        "cache_band_v": nrm(ks[5], (DEPTH, DEC_BATCH, band_rows, N_HEADS_BAND, HEAD_DIM), 1.0),
        "norm_mix_g": 1.0 + nrm(ks[6], (DEPTH, D_MODEL), 0.02),
        "w_in": nrm(ks[7], (DEPTH, D_MODEL, 3 * MIX_WIDTH), D_MODEL ** -0.5),
        "rel_bias": nrm(ks[8], (DEPTH, N_HEADS_BAND, 2 * REL_CLIP + 1), 0.1),
        "norm_sb_g": 1.0 + nrm(ks[9], (DEPTH, W_SB), 0.02),
        "norm_band_g": 1.0 + nrm(ks[10], (DEPTH, W_BAND), 0.02),
        "w_out": nrm(ks[11], (DEPTH, MIX_WIDTH, D_MODEL), MIX_WIDTH ** -0.5),
        "norm_ffn_g": 1.0 + nrm(ks[12], (DEPTH, D_MODEL), 0.02),
        "w_up": nrm(ks[13], (DEPTH, D_MODEL, D_FF), D_MODEL ** -0.5),
        "w_down": nrm(ks[14], (DEPTH, D_FF, D_MODEL), D_FF ** -0.5),
        "norm_final_g": 1.0 + nrm(ks[15], (D_MODEL,), 0.02),
    }


def reference(x_prompt, x_sample, cache_sb_k, cache_sb_v, cache_band_k, cache_band_v,
              norm_mix_g, w_in, rel_bias, norm_sb_g, norm_band_g, w_out,
              norm_ffn_g, w_up, w_down, norm_final_g):
    S = x_prompt.shape[1]
    T = x_sample.shape[1]
    past = cache_sb_k.shape[2]
    band_rows = cache_band_k.shape[2]
    keep_p = min(PAST_CHUNKS * CHUNK, S)

    h_p, h_s = x_prompt, x_sample
    sbk_p, sbv_p, bdk_p, bdv_p = [], [], [], []
    sbk_s, sbv_s, bdk_s, bdv_s = [], [], [], []
    for l in range(DEPTH):
        xn = rmsnorm(h_p, norm_mix_g[l])
        q_sb, k_sb, v_sb, q_bd, k_bd, v_bd = project(xn, w_in[l])
        o_sb = stick_breaking_prompt(q_sb, k_sb, v_sb)
        o_bd = band_prompt(q_bd, k_bd, v_bd, rel_bias[l])
        h_p = h_p + mix_out(o_sb, o_bd, norm_sb_g[l], norm_band_g[l], w_out[l])
        h_p = h_p + ffn(h_p, norm_ffn_g[l], w_up[l], w_down[l])
        sbk_p.append(k_sb)
        sbv_p.append(v_sb)
        bdk_p.append(k_bd[:, S - keep_p:])
        bdv_p.append(v_bd[:, S - keep_p:])

        xn = rmsnorm(h_s, norm_mix_g[l])
        q_sb, k_sb, v_sb, q_bd, k_bd, v_bd = project(xn, w_in[l])
        q_pos = past + jnp.arange(T)
        k_all = jnp.concatenate([cache_sb_k[l].astype(k_sb.dtype), k_sb], axis=1)
        v_all = jnp.concatenate([cache_sb_v[l].astype(v_sb.dtype), v_sb], axis=1)
        o_sb = stick_breaking(q_sb, k_all, v_all, q_pos)
        kb_all = jnp.concatenate([cache_band_k[l].astype(k_bd.dtype), k_bd], axis=1)
        vb_all = jnp.concatenate([cache_band_v[l].astype(v_bd.dtype), v_bd], axis=1)
        kb_pos = past - band_rows + jnp.arange(band_rows + T)
        o_bd = band_attend(q_bd, kb_all, vb_all, q_pos, kb_pos, rel_bias[l])
        h_s = h_s + mix_out(o_sb, o_bd, norm_sb_g[l], norm_band_g[l], w_out[l])
        h_s = h_s + ffn(h_s, norm_ffn_g[l], w_up[l], w_down[l])
        sbk_s.append(k_sb)
        sbv_s.append(v_sb)
        bdk_s.append(k_bd)
        bdv_s.append(v_bd)

    y_prompt = rmsnorm(h_p, norm_final_g)
    y_sample = rmsnorm(h_s, norm_final_g)
    sb_k_prompt = jnp.stack(sbk_p)
    sb_v_prompt = jnp.stack(sbv_p)
    band_k_prompt = jnp.stack(bdk_p)
    band_v_prompt = jnp.stack(bdv_p)
    sb_k_sample = jnp.stack(sbk_s)
    sb_v_sample = jnp.stack(sbv_s)
    band_k_sample = jnp.stack(bdk_s)
    band_v_sample = jnp.stack(bdv_s)
    return (y_prompt, y_sample, sb_k_prompt, sb_v_prompt, band_k_prompt, band_v_prompt,
            sb_k_sample, sb_v_sample, band_k_sample, band_v_sample)
```

```python
import functools
import math

import jax
import jax.numpy as jnp
from jax import lax
from jax.experimental import pallas as pl
from jax.experimental.pallas import tpu as pltpu

F32 = jnp.float32
BF16 = jnp.bfloat16

HEAD_DIM = 64
CHUNK = 64
PAST_CHUNKS = 8
REL_CLIP = 2 * CHUNK
EPS = 1e-6
NEG_INF = -1e30
LOG2E = 1.4426950408889634

LANES = 128
KEY_BLOCK = 256
ROW_TILE = 512
VMEM_LIMIT = 56 * 1024 * 1024

_NT = (((1,), (1,)), ((), ()))


def _rms(x, g):
    return x * lax.rsqrt(jnp.mean(x * x, axis=-1, keepdims=True) + EPS) * g


def _head_masks(shape):
    lane = lax.broadcasted_iota(jnp.int32, shape, 1)
    return lane < HEAD_DIM, lane >= HEAD_DIM


def _norm_proj_kernel(x_ref, g_ref, w_ref, qsb_ref, ksb_ref, vsb_ref, qbd_ref, kbd_ref, vbd_ref,
                      ksb32_ref, vsb32_ref, kbd32_ref, vbd32_ref, *, width, tail_period, sb_qscale,
                      bd_qscale):
    xn = _rms(x_ref[...], g_ref[...]).astype(BF16)

    def proj(c):
        return jnp.dot(xn, w_ref[:, c * width:(c + 1) * width], preferred_element_type=F32)

    qsb_ref[...] = (proj(0) * sb_qscale).astype(BF16)
    p = proj(1)
    ksb_ref[...] = p.astype(BF16)
    ksb32_ref[...] = p
    p = proj(2)
    vsb_ref[...] = p.astype(BF16)
    vsb32_ref[...] = p
    qbd_ref[...] = (proj(3) * bd_qscale).astype(BF16)
    pk = proj(4)
    kbd_ref[...] = pk.astype(BF16)
    pv = proj(5)
    vbd_ref[...] = pv.astype(BF16)

    def write_band32():
        kbd32_ref[...] = pk
        vbd32_ref[...] = pv

    if tail_period is None:
        write_band32()
    else:
        pl.when(pl.program_id(0) % tail_period == tail_period - 1)(write_band32)


def _norm_proj(x2, g, w_bf16, *, tail_period, sb_qscale, bd_qscale):
    n, d = x2.shape
    width = w_bf16.shape[1] // 6
    tm = min(ROW_TILE, n)
    assert n % tm == 0
    nt = n // tm
    row = lambda i: (i, 0)
    const = lambda i: (0, 0)
    if tail_period is None:
        band_rows, band_map = n, row
    else:
        assert nt % tail_period == 0
        band_rows, band_map = (nt // tail_period) * tm, (lambda i: (i // tail_period, 0))
    blk = lambda m: pl.BlockSpec((tm, width), m)
    out_shape = ([jax.ShapeDtypeStruct((n, width), BF16)] * 6
                 + [jax.ShapeDtypeStruct((n, width), F32)] * 2
                 + [jax.ShapeDtypeStruct((band_rows, width), F32)] * 2)
    return pl.pallas_call(
        functools.partial(_norm_proj_kernel, width=width, tail_period=tail_period,
                          sb_qscale=sb_qscale, bd_qscale=bd_qscale),
        out_shape=out_shape,
        grid=(nt,),
        in_specs=[pl.BlockSpec((tm, d), row), pl.BlockSpec((1, d), const),
                  pl.BlockSpec(w_bf16.shape, const)],
        out_specs=[blk(row)] * 8 + [blk(band_map)] * 2,
        compiler_params=pltpu.CompilerParams(dimension_semantics=("arbitrary",),
                                             vmem_limit_bytes=VMEM_LIMIT),
        name="norm_proj",
    )(x2, g.reshape(1, d), w_bf16)


def _sb_block(q, kblk, vblk, tri, carry, mask):
    z = lax.dot_general(q, kblk, _NT, preferred_element_type=F32)
    l2 = jnp.log2(1.0 + jnp.exp2(-jnp.abs(z)))
    sp = jnp.maximum(z, 0.0) + l2
    ls = jnp.minimum(z, 0.0) - l2
    if mask is not None:
        sp = jnp.where(mask, sp, 0.0)
    excl = jnp.dot(sp.astype(BF16), tri, preferred_element_type=F32)
    w = jnp.exp2(ls - excl - carry)
    if mask is not None:
        w = jnp.where(mask, w, 0.0)
    pv = jnp.dot(w.astype(BF16), vblk, preferred_element_type=F32)
    return pv, carry + jnp.sum(sp, axis=-1, keepdims=True)


def _sb_kernel(q_ref, kd_ref, vd_ref, kp_ref, vp_ref, tri_ref, o_ref, *, bq, n_past_static):
    n_past = pl.program_id(2) if n_past_static is None else n_past_static
    tri = tri_ref[...]
    q = q_ref[...]
    head_lanes = _head_masks(q.shape)
    row = lax.broadcasted_iota(jnp.int32, (bq, KEY_BLOCK), 0)
    col = lax.broadcasted_iota(jnp.int32, (bq, KEY_BLOCK), 1)
    causal = col < row
    accs = []
    for h in range(2):
        qh = jnp.where(head_lanes[h], q, jnp.zeros_like(q))
        acc, carry = _sb_block(qh, kd_ref[...].astype(BF16), vd_ref[...].astype(BF16), tri,
                               jnp.zeros((bq, 1), F32), causal)

        def body(t, c, qh=qh):
            acc, carry = c
            r0 = pl.multiple_of((n_past - 1 - t) * KEY_BLOCK, KEY_BLOCK)
            kblk = kp_ref[pl.ds(r0, KEY_BLOCK), :].astype(BF16)
            vblk = vp_ref[pl.ds(r0, KEY_BLOCK), :].astype(BF16)
            pv, carry = _sb_block(qh, kblk, vblk, tri, carry, None)
            return acc + pv, carry

        acc, carry = lax.fori_loop(0, n_past, body, (acc, carry))
        accs.append(acc)
    o_ref[...] = jnp.where(head_lanes[0], accs[0], accs[1])


def _sb_attn(q, kd, vd, kp, vp, *, bq, diag_follows_q, n_past_static):
    b, t, w = q.shape
    assert t % bq == 0 and bq <= KEY_BLOCK and w % LANES == 0
    assert kd.shape[1] % KEY_BLOCK == 0 and kp.shape[1] % KEY_BLOCK == 0
    tri = (lax.broadcasted_iota(jnp.int32, (KEY_BLOCK, KEY_BLOCK), 0)
           > lax.broadcasted_iota(jnp.int32, (KEY_BLOCK, KEY_BLOCK), 1)).astype(BF16)
    qmap = lambda bi, hp, i: (bi, i, hp)
    dmap = qmap if diag_follows_q else (lambda bi, hp, i: (bi, 0, hp))
    pmap = lambda bi, hp, i: (bi, 0, hp)
    return pl.pallas_call(
        functools.partial(_sb_kernel, bq=bq, n_past_static=n_past_static),
        out_shape=jax.ShapeDtypeStruct((b, t, w), F32),
        grid=(b, w // LANES, t // bq),
        in_specs=[pl.BlockSpec((None, bq, LANES), qmap),
                  pl.BlockSpec((None, KEY_BLOCK, LANES), dmap),
                  pl.BlockSpec((None, KEY_BLOCK, LANES), dmap),
                  pl.BlockSpec((None, kp.shape[1], LANES), pmap),
                  pl.BlockSpec((None, vp.shape[1], LANES), pmap),
                  pl.BlockSpec((KEY_BLOCK, KEY_BLOCK), lambda bi, hp, i: (0, 0))],
        out_specs=pl.BlockSpec((None, bq, LANES), qmap),
        compiler_params=pltpu.CompilerParams(
            dimension_semantics=("parallel", "parallel", "arbitrary"),
            vmem_limit_bytes=VMEM_LIMIT),
        name="sb_attn",
    )(q, kd, vd, kp, vp, tri)


def _band_kernel(q_ref, k_ref, v_ref, bias_ref, o_ref, *, bq, kw, back):
    ks = pl.multiple_of(jnp.maximum(pl.program_id(2) * bq - back, 0), bq)
    q = q_ref[...]
    kblk = k_ref[pl.ds(ks, kw), :].astype(BF16)
    vblk = v_ref[pl.ds(ks, kw), :].astype(BF16)
    head_lanes = _head_masks(q.shape)
    outs = []
    for h in range(2):
        qh = jnp.where(head_lanes[h], q, jnp.zeros_like(q))
        s = lax.dot_general(qh, kblk, _NT, preferred_element_type=F32) + bias_ref[h]
        p = jnp.exp(s - jnp.max(s, axis=-1, keepdims=True))
        den = jnp.sum(p, axis=-1, keepdims=True)
        outs.append(jnp.dot(p.astype(BF16), vblk, preferred_element_type=F32) / den)
    o_ref[...] = jnp.where(head_lanes[0], outs[0], outs[1])


def _band_attn(q, k, v, bias, *, bq, kw, back):
    b, t, w = q.shape
    nvar = bias.shape[0]
    assert t % bq == 0 and w % LANES == 0
    return pl.pallas_call(
        functools.partial(_band_kernel, bq=bq, kw=kw, back=back),
        out_shape=jax.ShapeDtypeStruct((b, t, w), F32),
        grid=(b, w // LANES, t // bq),
        in_specs=[pl.BlockSpec((None, bq, LANES), lambda bi, hp, i: (bi, i, hp)),
                  pl.BlockSpec((None, k.shape[1], LANES), lambda bi, hp, i: (bi, 0, hp)),
                  pl.BlockSpec((None, v.shape[1], LANES), lambda bi, hp, i: (bi, 0, hp)),
                  pl.BlockSpec((None, 2, bq, kw),
                               lambda bi, hp, i: (jnp.minimum(i, nvar - 1), hp, 0, 0))],
        out_specs=pl.BlockSpec((None, bq, LANES), lambda bi, hp, i: (bi, i, hp)),
        compiler_params=pltpu.CompilerParams(
            dimension_semantics=("parallel", "parallel", "arbitrary"),
            vmem_limit_bytes=VMEM_LIMIT),
        name="band_attn",
    )(q, k, v, bias)


def _band_bias(rel_bias, q_pos, k_pos, valid):
    rel = jnp.clip(q_pos[:, None] - k_pos[None, :], -REL_CLIP, REL_CLIP) + REL_CLIP
    return jnp.where(valid[None], rel_bias.astype(F32)[:, rel], NEG_INF)


def _prompt_band_bias(rel_bias, bq, kw, back):
    tiles = []
    r = jnp.arange(bq)
    j = jnp.arange(kw)
    for var in range(back // bq + 1):
        q_pos = var * bq + r
        qc = q_pos[:, None] // CHUNK
        kc = j[None, :] // CHUNK
        valid = (kc <= qc) & (kc >= qc - PAST_CHUNKS)
        tiles.append(_band_bias(rel_bias, q_pos, j, valid))
    return jnp.stack(tiles)


def _out_ffn_kernel(h_ref, osb_ref, obd_ref, gsb_ref, gbd_ref, wout_ref, gffn_ref, wup_ref, wdn_ref,
                    gfin_ref, y_ref, *, w_sb, ff_chunk, apply_final):
    ysb = _rms(osb_ref[...], gsb_ref[...]).astype(BF16)
    ybd = _rms(obd_ref[...], gbd_ref[...]).astype(BF16)
    h1 = (h_ref[...]
          + jnp.dot(ysb, wout_ref[:w_sb, :], preferred_element_type=F32)
          + jnp.dot(ybd, wout_ref[w_sb:, :], preferred_element_type=F32))
    xn = _rms(h1, gffn_ref[...]).astype(BF16)
    mlp = None
    for c in range(wup_ref.shape[1] // ff_chunk):
        u = jnp.dot(xn, wup_ref[:, c * ff_chunk:(c + 1) * ff_chunk], preferred_element_type=F32)
        a = jnp.square(jnp.maximum(u, 0.0)).astype(BF16)
        d = jnp.dot(a, wdn_ref[c * ff_chunk:(c + 1) * ff_chunk, :], preferred_element_type=F32)
        mlp = d if mlp is None else mlp + d
    h2 = h1 + mlp
    y_ref[...] = _rms(h2, gfin_ref[...]) if apply_final else h2


def _out_ffn(h2, osb, obd, g_sb, g_bd, w_out, g_ffn, w_up, w_dn, g_fin, *, apply_final):
    n, d = h2.shape
    w_sb, w_bd = osb.shape[1], obd.shape[1]
    tm = min(ROW_TILE, n)
    assert n % tm == 0
    row = lambda i: (i, 0)
    const = lambda i: (0, 0)
    resident = lambda a: pl.BlockSpec(a.shape, const, pipeline_mode=pl.Buffered(1))
    vec = lambda a: a.reshape(1, -1).astype(F32)
    args = (h2, osb, obd, vec(g_sb), vec(g_bd), w_out, vec(g_ffn), w_up, w_dn, vec(g_fin))
    in_specs = [pl.BlockSpec((tm, d), row), pl.BlockSpec((tm, w_sb), row),
                pl.BlockSpec((tm, w_bd), row)] + [resident(a) for a in args[3:]]
    return pl.pallas_call(
        functools.partial(_out_ffn_kernel, w_sb=w_sb, ff_chunk=min(1024, w_up.shape[1]),
                          apply_final=apply_final),
        out_shape=jax.ShapeDtypeStruct((n, d), F32),
        grid=(n // tm,),
        in_specs=in_specs,
        out_specs=pl.BlockSpec((tm, d), row),
        compiler_params=pltpu.CompilerParams(dimension_semantics=("arbitrary",),
                                             vmem_limit_bytes=VMEM_LIMIT),
        name="out_ffn",
    )(*args)


def _pad_rows(a, rows):
    return jnp.pad(a, ((0, 0), (0, rows - a.shape[1]), (0, 0)))


def kernel(x_prompt, x_sample, cache_sb_k, cache_sb_v, cache_band_k, cache_band_v, norm_mix_g, w_in,
           rel_bias, norm_sb_g, norm_band_g, w_out, norm_ffn_g, w_up, w_down, norm_final_g):
    depth = w_in.shape[0]
    bsz, seq, d_model = x_prompt.shape
    dbsz, dseq, _ = x_sample.shape
    past = cache_sb_k.shape[2]
    band_rows = cache_band_k.shape[2]
    n_sb, n_bd = cache_sb_k.shape[3], cache_band_k.shape[3]
    w_sb, w_bd = n_sb * HEAD_DIM, n_bd * HEAD_DIM
    keep_p = min(PAST_CHUNKS * CHUNK, seq)
    back = PAST_CHUNKS * CHUNK
    assert w_sb == w_bd and w_in.shape[2] == 6 * w_sb
    assert seq % KEY_BLOCK == 0 and seq >= back + KEY_BLOCK and keep_p == ROW_TILE
    assert dseq <= KEY_BLOCK and past % KEY_BLOCK == 0 and band_rows == back
    sb_qscale = LOG2E / math.sqrt(HEAD_DIM)
    bd_qscale = 1.0 / math.sqrt(HEAD_DIM)

    h_p, h_s = x_prompt, x_sample
    outs = [[] for _ in range(8)]
    for l in range(depth):
        w_in_l = w_in[l].astype(BF16)
        w_out_l, w_up_l, w_dn_l = (w[l].astype(BF16) for w in (w_out, w_up, w_down))
        last = l == depth - 1
        tail = dict(g_sb=norm_sb_g[l], g_bd=norm_band_g[l], w_out=w_out_l, g_ffn=norm_ffn_g[l],
                    w_up=w_up_l, w_dn=w_dn_l, g_fin=norm_final_g, apply_final=last)

        n = bsz * seq
        (qsb, ksb, vsb, qbd, kbd, vbd, ksb32, vsb32, kbd32, vbd32) = _norm_proj(
            h_p.reshape(n, d_model), norm_mix_g[l], w_in_l, tail_period=seq // ROW_TILE,
            sb_qscale=sb_qscale, bd_qscale=bd_qscale)
        r3 = lambda a: a.reshape(bsz, -1, a.shape[-1])
        qsb, ksb, vsb, qbd, kbd, vbd = map(r3, (qsb, ksb, vsb, qbd, kbd, vbd))
        o_sb = _sb_attn(qsb, ksb, vsb, ksb, vsb, bq=KEY_BLOCK, diag_follows_q=True,
                        n_past_static=None)
        kw = back + KEY_BLOCK
        o_bd = _band_attn(qbd, kbd, vbd, _prompt_band_bias(rel_bias[l], KEY_BLOCK, kw, back),
                          bq=KEY_BLOCK, kw=kw, back=back)
        h_p = _out_ffn(h_p.reshape(n, d_model), o_sb.reshape(n, w_sb), o_bd.reshape(n, w_bd),
                       **tail).reshape(bsz, seq, d_model)
        outs[0].append(ksb32.reshape(bsz, seq, n_sb, HEAD_DIM))
        outs[1].append(vsb32.reshape(bsz, seq, n_sb, HEAD_DIM))
        outs[2].append(kbd32.reshape(bsz, keep_p, n_bd, HEAD_DIM))
        outs[3].append(vbd32.reshape(bsz, keep_p, n_bd, HEAD_DIM))

        n = dbsz * dseq
        (qsb, ksb, vsb, qbd, kbd, vbd, ksb32, vsb32, kbd32, vbd32) = _norm_proj(
            h_s.reshape(n, d_model), norm_mix_g[l], w_in_l, tail_period=None,
            sb_qscale=sb_qscale, bd_qscale=bd_qscale)
        r3 = lambda a: a.reshape(dbsz, -1, a.shape[-1])
        qsb, ksb, vsb, qbd, kbd32, vbd32 = map(r3, (qsb, ksb, vsb, qbd, kbd32, vbd32))
        o_sb = _sb_attn(qsb, _pad_rows(ksb, KEY_BLOCK), _pad_rows(vsb, KEY_BLOCK),
                        cache_sb_k[l].reshape(dbsz, past, w_sb),
                        cache_sb_v[l].reshape(dbsz, past, w_sb),
                        bq=dseq, diag_follows_q=False, n_past_static=past // KEY_BLOCK)
        kw = -(-(band_rows + dseq) // LANES) * LANES
        kb_all = _pad_rows(jnp.concatenate(
            [cache_band_k[l].reshape(dbsz, band_rows, w_bd), kbd32], axis=1), kw)
        vb_all = _pad_rows(jnp.concatenate(
            [cache_band_v[l].reshape(dbsz, band_rows, w_bd), vbd32], axis=1), kw)
        k_pos = past - band_rows + jnp.arange(kw)
        valid = (k_pos >= 0) & (k_pos < past + dseq)
        bias = _band_bias(rel_bias[l], past + jnp.arange(dseq), k_pos,
                          jnp.broadcast_to(valid[None, :], (dseq, kw)))[None]
        o_bd = _band_attn(qbd, kb_all, vb_all, bias, bq=dseq, kw=kw, back=0)
        h_s = _out_ffn(h_s.reshape(n, d_model), o_sb.reshape(n, w_sb), o_bd.reshape(n, w_bd),
                       **tail).reshape(dbsz, dseq, d_model)
        outs[4].append(ksb32.reshape(dbsz, dseq, n_sb, HEAD_DIM))
        outs[5].append(vsb32.reshape(dbsz, dseq, n_sb, HEAD_DIM))
        outs[6].append(kbd32.reshape(dbsz, dseq, n_bd, HEAD_DIM))
        outs[7].append(vbd32.reshape(dbsz, dseq, n_bd, HEAD_DIM))

    return (h_p, h_s) + tuple(jnp.stack(o) for o in outs)
```

```python
import functools
import math

import jax
import jax.numpy as jnp
from jax import lax
from jax.experimental import pallas as pl
from jax.experimental.pallas import tpu as pltpu

F32 = jnp.float32
BF16 = jnp.bfloat16

HEAD_DIM = 64
CHUNK = 64
PAST_CHUNKS = 8
REL_CLIP = 2 * CHUNK
EPS = 1e-6
NEG_INF = -1e30
LOG2E = 1.4426950408889634
SB_CUTOFF = 150.0

LANES = 128
KEY_BLOCK = 256
ROW_TILE = 512
VMEM_LIMIT = 56 * 1024 * 1024

_NT = (((1,), (1,)), ((), ()))


def _rms(x, g):
    return x * lax.rsqrt(jnp.mean(x * x, axis=-1, keepdims=True) + EPS) * g


def _head_masks(shape):
    lane = lax.broadcasted_iota(jnp.int32, shape, 1)
    return lane < HEAD_DIM, lane >= HEAD_DIM


def _norm_proj_kernel(x_ref, g_ref, w_ref, qsb_ref, ksb_ref, vsb_ref, qbd_ref, kbd_ref, vbd_ref,
                      ksb32_ref, vsb32_ref, kbd32_ref, vbd32_ref, *, width, tail_period, sb_qscale,
                      bd_qscale):
    xn = _rms(x_ref[...], g_ref[...]).astype(BF16)

    def proj(c):
        return jnp.dot(xn, w_ref[:, c * width:(c + 1) * width], preferred_element_type=F32)

    qsb_ref[...] = (proj(0) * sb_qscale).astype(BF16)
    p = proj(1)
    ksb_ref[...] = p.astype(BF16)
    ksb32_ref[...] = p
    p = proj(2)
    vsb_ref[...] = p.astype(BF16)
    vsb32_ref[...] = p
    qbd_ref[...] = (proj(3) * bd_qscale).astype(BF16)
    pk = proj(4)
    kbd_ref[...] = pk.astype(BF16)
    pv = proj(5)
    vbd_ref[...] = pv.astype(BF16)

    def write_band32():
        kbd32_ref[...] = pk
        vbd32_ref[...] = pv

    if tail_period is None:
        write_band32()
    else:
        pl.when(pl.program_id(0) % tail_period == tail_period - 1)(write_band32)


def _norm_proj(x2, g, w_bf16, *, tail_period, sb_qscale, bd_qscale):
    n, d = x2.shape
    width = w_bf16.shape[1] // 6
    tm = min(ROW_TILE, n)
    assert n % tm == 0
    nt = n // tm
    row = lambda i: (i, 0)
    const = lambda i: (0, 0)
    if tail_period is None:
        band_rows, band_map = n, row
    else:
        assert nt % tail_period == 0
        band_rows, band_map = (nt // tail_period) * tm, (lambda i: (i // tail_period, 0))
    blk = lambda m: pl.BlockSpec((tm, width), m)
    out_shape = ([jax.ShapeDtypeStruct((n, width), BF16)] * 6
                 + [jax.ShapeDtypeStruct((n, width), F32)] * 2
                 + [jax.ShapeDtypeStruct((band_rows, width), F32)] * 2)
    return pl.pallas_call(
        functools.partial(_norm_proj_kernel, width=width, tail_period=tail_period,
                          sb_qscale=sb_qscale, bd_qscale=bd_qscale),
        out_shape=out_shape,
        grid=(nt,),
        in_specs=[pl.BlockSpec((tm, d), row), pl.BlockSpec((1, d), const),
                  pl.BlockSpec(w_bf16.shape, const)],
        out_specs=[blk(row)] * 8 + [blk(band_map)] * 2,
        compiler_params=pltpu.CompilerParams(dimension_semantics=("arbitrary",),
                                             vmem_limit_bytes=VMEM_LIMIT),
        name="norm_proj",
    )(x2, g.reshape(1, d), w_bf16)


def _sb_block(q, kblk, vblk, tri, carry, mask):
    z = lax.dot_general(q, kblk, _NT, preferred_element_type=F32)
    neg_abs = pltpu.bitcast(pltpu.bitcast(z, jnp.uint32) | jnp.uint32(0x80000000), F32)
    sp = jnp.maximum(z, 0.0) + jnp.log2(1.0 + jnp.exp2(neg_abs))
    if mask is not None:
        sp = jnp.where(mask, sp, 0.0)
    excl = jnp.dot(sp.astype(BF16), tri, preferred_element_type=F32)
    t = sp + excl if carry is None else sp + excl + carry
    w = jnp.exp2(z - t)
    if mask is not None:
        w = jnp.where(mask, w, 0.0)
    pv = jnp.dot(w.astype(BF16), vblk, preferred_element_type=F32)
    rs = jnp.sum(sp, axis=-1, keepdims=True)
    return pv, rs if carry is None else carry + rs


def _sb_kernel(q_ref, kd_ref, vd_ref, kp_ref, vp_ref, tri_ref, o_ref, acc_ref, *, bq,
               n_past_static):
    n_past = pl.program_id(2) if n_past_static is None else n_past_static
    tri = tri_ref[...]
    q = q_ref[...]
    head_lanes = _head_masks(q.shape)
    qh = [jnp.where(m, q, jnp.zeros_like(q)) for m in head_lanes]
    row = lax.broadcasted_iota(jnp.int32, (bq, KEY_BLOCK), 0)
    col = lax.broadcasted_iota(jnp.int32, (bq, KEY_BLOCK), 1)
    causal = col < row
    kd = kd_ref[...].astype(BF16)
    vd = vd_ref[...].astype(BF16)
    r1 = pl.multiple_of(jnp.maximum(n_past - 1, 0) * KEY_BLOCK, KEY_BLOCK)
    k1 = kp_ref[pl.ds(r1, KEY_BLOCK), :].astype(BF16)
    v1 = vp_ref[pl.ds(r1, KEY_BLOCK), :].astype(BF16)
    no_past = jnp.where(n_past > 0, 0.0, -NEG_INF).astype(F32)
    carries = []
    for h in range(2):
        pvd, c = _sb_block(qh[h], kd, vd, tri, None, causal)
        pv1, c = _sb_block(qh[h], k1, v1, tri, c + no_past, None)
        acc_ref[h] = pvd + pv1
        carries.append(c)

    def remaining(c0, c1):
        return jnp.min(jnp.minimum(c0, c1))

    def cond(state):
        j, least, _, _ = state
        return jnp.logical_and(j >= 0, least < SB_CUTOFF)

    def body(state):
        j, _, c0, c1 = state
        r0 = pl.multiple_of(j * KEY_BLOCK, KEY_BLOCK)
        kblk = kp_ref[pl.ds(r0, KEY_BLOCK), :].astype(BF16)
        vblk = vp_ref[pl.ds(r0, KEY_BLOCK), :].astype(BF16)
        pv0, c0 = _sb_block(qh[0], kblk, vblk, tri, c0, None)
        pv1, c1 = _sb_block(qh[1], kblk, vblk, tri, c1, None)
        acc_ref[0] += pv0
        acc_ref[1] += pv1
        return j - 1, remaining(c0, c1), c0, c1

    lax.while_loop(cond, body, (n_past - 2, remaining(*carries), carries[0], carries[1]))
    o_ref[...] = jnp.where(head_lanes[0], acc_ref[0], acc_ref[1])


def _sb_attn(q, kd, vd, kp, vp, *, bq, diag_follows_q, n_past_static):
    b, t, w = q.shape
    assert t % bq == 0 and bq <= KEY_BLOCK and w % LANES == 0
    assert kd.shape[1] % KEY_BLOCK == 0 and kp.shape[1] % KEY_BLOCK == 0
    tri = (lax.broadcasted_iota(jnp.int32, (KEY_BLOCK, KEY_BLOCK), 0)
           > lax.broadcasted_iota(jnp.int32, (KEY_BLOCK, KEY_BLOCK), 1)).astype(BF16)
    qmap = lambda bi, hp, i: (bi, i, hp)
    dmap = qmap if diag_follows_q else (lambda bi, hp, i: (bi, 0, hp))
    pmap = lambda bi, hp, i: (bi, 0, hp)
    return pl.pallas_call(
        functools.partial(_sb_kernel, bq=bq, n_past_static=n_past_static),
        out_shape=jax.ShapeDtypeStruct((b, t, w), F32),
        grid=(b, w // LANES, t // bq),
        in_specs=[pl.BlockSpec((None, bq, LANES), qmap),
                  pl.BlockSpec((None, KEY_BLOCK, LANES), dmap),
                  pl.BlockSpec((None, KEY_BLOCK, LANES), dmap),
                  pl.BlockSpec((None, kp.shape[1], LANES), pmap),
                  pl.BlockSpec((None, vp.shape[1], LANES), pmap),
                  pl.BlockSpec((KEY_BLOCK, KEY_BLOCK), lambda bi, hp, i: (0, 0))],
        out_specs=pl.BlockSpec((None, bq, LANES), qmap),
        compiler_params=pltpu.CompilerParams(
            dimension_semantics=("parallel", "parallel", "arbitrary"),
            vmem_limit_bytes=VMEM_LIMIT),
        scratch_shapes=[pltpu.VMEM((2, bq, LANES), F32)],
        name="sb_attn",
    )(q, kd, vd, kp, vp, tri)


def _band_kernel(q_ref, k_ref, v_ref, bias_ref, o_ref, *, bq, kw, back):
    ks = pl.multiple_of(jnp.maximum(pl.program_id(2) * bq - back, 0), bq)
    q = q_ref[...]
    kblk = k_ref[pl.ds(ks, kw), :].astype(BF16)
    vblk = v_ref[pl.ds(ks, kw), :].astype(BF16)
    head_lanes = _head_masks(q.shape)
    outs = []
    for h in range(2):
        qh = jnp.where(head_lanes[h], q, jnp.zeros_like(q))
        s = lax.dot_general(qh, kblk, _NT, preferred_element_type=F32) + bias_ref[h]
        p = jnp.exp(s - jnp.max(s, axis=-1, keepdims=True))
        den = jnp.sum(p, axis=-1, keepdims=True)
        outs.append(jnp.dot(p.astype(BF16), vblk, preferred_element_type=F32) / den)
    o_ref[...] = jnp.where(head_lanes[0], outs[0], outs[1])


def _band_attn(q, k, v, bias, *, bq, kw, back):
    b, t, w = q.shape
    nvar = bias.shape[0]
    assert t % bq == 0 and w % LANES == 0
    return pl.pallas_call(
        functools.partial(_band_kernel, bq=bq, kw=kw, back=back),
        out_shape=jax.ShapeDtypeStruct((b, t, w), F32),
        grid=(b, w // LANES, t // bq),
        in_specs=[pl.BlockSpec((None, bq, LANES), lambda bi, hp, i: (bi, i, hp)),
                  pl.BlockSpec((None, k.shape[1], LANES), lambda bi, hp, i: (bi, 0, hp)),
                  pl.BlockSpec((None, v.shape[1], LANES), lambda bi, hp, i: (bi, 0, hp)),
                  pl.BlockSpec((None, 2, bq, kw),
                               lambda bi, hp, i: (jnp.minimum(i, nvar - 1), hp, 0, 0))],
        out_specs=pl.BlockSpec((None, bq, LANES), lambda bi, hp, i: (bi, i, hp)),
        compiler_params=pltpu.CompilerParams(
            dimension_semantics=("parallel", "parallel", "arbitrary"),
            vmem_limit_bytes=VMEM_LIMIT),
        name="band_attn",
    )(q, k, v, bias)


def _band_bias(rel_bias, q_pos, k_pos, valid):
    rel = jnp.clip(q_pos[:, None] - k_pos[None, :], -REL_CLIP, REL_CLIP) + REL_CLIP
    return jnp.where(valid[None], rel_bias.astype(F32)[:, rel], NEG_INF)


def _prompt_band_bias(rel_bias, bq, kw, back):
    tiles = []
    r = jnp.arange(bq)
    j = jnp.arange(kw)
    for var in range(back // bq + 1):
        q_pos = var * bq + r
        qc = q_pos[:, None] // CHUNK
        kc = j[None, :] // CHUNK
        valid = (kc <= qc) & (kc >= qc - PAST_CHUNKS)
        tiles.append(_band_bias(rel_bias, q_pos, j, valid))
    return jnp.stack(tiles)


def _out_ffn_kernel(h_ref, osb_ref, obd_ref, gsb_ref, gbd_ref, wout_ref, gffn_ref, wup_ref, wdn_ref,
                    gfin_ref, y_ref, *, w_sb, ff_chunk, apply_final):
    ysb = _rms(osb_ref[...], gsb_ref[...]).astype(BF16)
    ybd = _rms(obd_ref[...], gbd_ref[...]).astype(BF16)
    h1 = (h_ref[...]
          + jnp.dot(ysb, wout_ref[:w_sb, :], preferred_element_type=F32)
          + jnp.dot(ybd, wout_ref[w_sb:, :], preferred_element_type=F32))
    xn = _rms(h1, gffn_ref[...]).astype(BF16)
    mlp = None
    for c in range(wup_ref.shape[1] // ff_chunk):
        u = jnp.dot(xn, wup_ref[:, c * ff_chunk:(c + 1) * ff_chunk], preferred_element_type=F32)
        a = jnp.square(jnp.maximum(u, 0.0)).astype(BF16)
        d = jnp.dot(a, wdn_ref[c * ff_chunk:(c + 1) * ff_chunk, :], preferred_element_type=F32)
        mlp = d if mlp is None else mlp + d
    h2 = h1 + mlp
    y_ref[...] = _rms(h2, gfin_ref[...]) if apply_final else h2


def _out_ffn(h2, osb, obd, g_sb, g_bd, w_out, g_ffn, w_up, w_dn, g_fin, *, apply_final):
    n, d = h2.shape
    w_sb, w_bd = osb.shape[1], obd.shape[1]
    tm = min(ROW_TILE, n)
    assert n % tm == 0
    row = lambda i: (i, 0)
    const = lambda i: (0, 0)
    resident = lambda a: pl.BlockSpec(a.shape, const, pipeline_mode=pl.Buffered(1))
    vec = lambda a: a.reshape(1, -1).astype(F32)
    args = (h2, osb, obd, vec(g_sb), vec(g_bd), w_out, vec(g_ffn), w_up, w_dn, vec(g_fin))
    in_specs = [pl.BlockSpec((tm, d), row), pl.BlockSpec((tm, w_sb), row),
                pl.BlockSpec((tm, w_bd), row)] + [resident(a) for a in args[3:]]
    return pl.pallas_call(
        functools.partial(_out_ffn_kernel, w_sb=w_sb, ff_chunk=min(1024, w_up.shape[1]),
                          apply_final=apply_final),
        out_shape=jax.ShapeDtypeStruct((n, d), F32),
        grid=(n // tm,),
        in_specs=in_specs,
        out_specs=pl.BlockSpec((tm, d), row),
        compiler_params=pltpu.CompilerParams(dimension_semantics=("arbitrary",),
                                             vmem_limit_bytes=VMEM_LIMIT),
        name="out_ffn",
    )(*args)


def _pad_rows(a, rows):
    return jnp.pad(a, ((0, 0), (0, rows - a.shape[1]), (0, 0)))


def kernel(x_prompt, x_sample, cache_sb_k, cache_sb_v, cache_band_k, cache_band_v, norm_mix_g, w_in,
           rel_bias, norm_sb_g, norm_band_g, w_out, norm_ffn_g, w_up, w_down, norm_final_g):
    depth = w_in.shape[0]
    bsz, seq, d_model = x_prompt.shape
    dbsz, dseq, _ = x_sample.shape
    past = cache_sb_k.shape[2]
    band_rows = cache_band_k.shape[2]
    n_sb, n_bd = cache_sb_k.shape[3], cache_band_k.shape[3]
    w_sb, w_bd = n_sb * HEAD_DIM, n_bd * HEAD_DIM
    keep_p = min(PAST_CHUNKS * CHUNK, seq)
    back = PAST_CHUNKS * CHUNK
    assert w_sb == w_bd and w_in.shape[2] == 6 * w_sb
    assert seq % KEY_BLOCK == 0 and seq >= back + KEY_BLOCK and keep_p == ROW_TILE
    assert dseq <= KEY_BLOCK and past % KEY_BLOCK == 0 and band_rows == back
    sb_qscale = LOG2E / math.sqrt(HEAD_DIM)
    bd_qscale = 1.0 / math.sqrt(HEAD_DIM)

    h_p, h_s = x_prompt, x_sample
    outs = [[] for _ in range(8)]
    for l in range(depth):
        w_in_l = w_in[l].astype(BF16)
        w_out_l, w_up_l, w_dn_l = (w[l].astype(BF16) for w in (w_out, w_up, w_down))
        last = l == depth - 1
        tail = dict(g_sb=norm_sb_g[l], g_bd=norm_band_g[l], w_out=w_out_l, g_ffn=norm_ffn_g[l],
                    w_up=w_up_l, w_dn=w_dn_l, g_fin=norm_final_g, apply_final=last)

        n = bsz * seq
        (qsb, ksb, vsb, qbd, kbd, vbd, ksb32, vsb32, kbd32, vbd32) = _norm_proj(
            h_p.reshape(n, d_model), norm_mix_g[l], w_in_l, tail_period=seq // ROW_TILE,
            sb_qscale=sb_qscale, bd_qscale=bd_qscale)
        r3 = lambda a: a.reshape(bsz, -1, a.shape[-1])
        qsb, ksb, vsb, qbd, kbd, vbd = map(r3, (qsb, ksb, vsb, qbd, kbd, vbd))
        o_sb = _sb_attn(qsb, ksb, vsb, ksb, vsb, bq=KEY_BLOCK, diag_follows_q=True,
                        n_past_static=None)
        kw = back + KEY_BLOCK
        o_bd = _band_attn(qbd, kbd, vbd, _prompt_band_bias(rel_bias[l], KEY_BLOCK, kw, back),
                          bq=KEY_BLOCK, kw=kw, back=back)
        h_p = _out_ffn(h_p.reshape(n, d_model), o_sb.reshape(n, w_sb), o_bd.reshape(n, w_bd),
                       **tail).reshape(bsz, seq, d_model)
        outs[0].append(ksb32.reshape(bsz, seq, n_sb, HEAD_DIM))
        outs[1].append(vsb32.reshape(bsz, seq, n_sb, HEAD_DIM))
        outs[2].append(kbd32.reshape(bsz, keep_p, n_bd, HEAD_DIM))
        outs[3].append(vbd32.reshape(bsz, keep_p, n_bd, HEAD_DIM))

        n = dbsz * dseq
        (qsb, ksb, vsb, qbd, kbd, vbd, ksb32, vsb32, kbd32, vbd32) = _norm_proj(
            h_s.reshape(n, d_model), norm_mix_g[l], w_in_l, tail_period=None,
            sb_qscale=sb_qscale, bd_qscale=bd_qscale)
        r3 = lambda a: a.reshape(dbsz, -1, a.shape[-1])
        qsb, ksb, vsb, qbd, kbd32, vbd32 = map(r3, (qsb, ksb, vsb, qbd, kbd32, vbd32))
        o_sb = _sb_attn(qsb, _pad_rows(ksb, KEY_BLOCK), _pad_rows(vsb, KEY_BLOCK),
                        cache_sb_k[l].reshape(dbsz, past, w_sb),
                        cache_sb_v[l].reshape(dbsz, past, w_sb),
                        bq=dseq, diag_follows_q=False, n_past_static=past // KEY_BLOCK)
        kw = -(-(band_rows + dseq) // LANES) * LANES
        kb_all = _pad_rows(jnp.concatenate(
            [cache_band_k[l].reshape(dbsz, band_rows, w_bd), kbd32], axis=1), kw)
        vb_all = _pad_rows(jnp.concatenate(
            [cache_band_v[l].reshape(dbsz, band_rows, w_bd), vbd32], axis=1), kw)
        k_pos = past - band_rows + jnp.arange(kw)
        valid = (k_pos >= 0) & (k_pos < past + dseq)
        bias = _band_bias(rel_bias[l], past + jnp.arange(dseq), k_pos,
                          jnp.broadcast_to(valid[None, :], (dseq, kw)))[None]
        o_bd = _band_attn(qbd, kb_all, vb_all, bias, bq=dseq, kw=kw, back=0)
        h_s = _out_ffn(h_s.reshape(n, d_model), o_sb.reshape(n, w_sb), o_bd.reshape(n, w_bd),
                       **tail).reshape(dbsz, dseq, d_model)
        outs[4].append(ksb32.reshape(dbsz, dseq, n_sb, HEAD_DIM))
        outs[5].append(vsb32.reshape(dbsz, dseq, n_sb, HEAD_DIM))
        outs[6].append(kbd32.reshape(dbsz, dseq, n_bd, HEAD_DIM))
        outs[7].append(vbd32.reshape(dbsz, dseq, n_bd, HEAD_DIM))

    return (h_p, h_s) + tuple(jnp.stack(o) for o in outs)
```

```python
import functools
import math

import jax
import jax.numpy as jnp
from jax import lax
from jax.experimental import pallas as pl
from jax.experimental.pallas import tpu as pltpu

F32 = jnp.float32
BF16 = jnp.bfloat16

HEAD_DIM = 64
CHUNK = 64
PAST_CHUNKS = 8
REL_CLIP = 2 * CHUNK
EPS = 1e-6
NEG_INF = -1e30
LOG2E = 1.4426950408889634
SB_CUTOFF = 150.0
SB_EXP_CLAMP = 64.0

LANES = 128
KEY_BLOCK = 256
ROW_TILE = 512
BAND_SUB_BLOCKS = 4
SB_SUB_BLOCKS = 4
VMEM_LIMIT = 56 * 1024 * 1024

_NT = (((1,), (1,)), ((), ()))


def _rms(x, g):
    return x * lax.rsqrt(jnp.mean(x * x, axis=-1, keepdims=True) + EPS) * g


def _head_masks(shape):
    lane = lax.broadcasted_iota(jnp.int32, shape, 1)
    return lane < HEAD_DIM, lane >= HEAD_DIM


def _norm_proj_kernel(x_ref, g_ref, w_ref, qsb_ref, ksb_ref, vsb_ref, qbd_ref, kbd_ref, vbd_ref,
                      ksb32_ref, vsb32_ref, kbd32_ref, vbd32_ref, *, width, tail_period, sb_qscale,
                      bd_qscale):
    xn = _rms(x_ref[...], g_ref[...]).astype(BF16)

    def proj(c):
        return jnp.dot(xn, w_ref[:, c * width:(c + 1) * width], preferred_element_type=F32)

    def split_heads(p):
        return pltpu.einshape("t(hd)->thd", p, d=HEAD_DIM)

    qsb_ref[...] = (proj(0) * sb_qscale).astype(BF16)
    p = proj(1)
    ksb_ref[...] = p.astype(BF16)
    ksb32_ref[...] = split_heads(p)
    p = proj(2)
    vsb_ref[...] = p.astype(BF16)
    vsb32_ref[...] = split_heads(p)
    qbd_ref[...] = (proj(3) * bd_qscale).astype(BF16)
    pk = proj(4)
    kbd_ref[...] = pk.astype(BF16)
    pv = proj(5)
    vbd_ref[...] = pv.astype(BF16)

    def write_band32():
        kbd32_ref[...] = split_heads(pk)
        vbd32_ref[...] = split_heads(pv)

    if tail_period is None:
        write_band32()
    else:
        pl.when(pl.program_id(0) % tail_period == tail_period - 1)(write_band32)


def _norm_proj(x2, g, w_bf16, *, tail_period, sb_qscale, bd_qscale):
    n, d = x2.shape
    width = w_bf16.shape[1] // 6
    tm = min(ROW_TILE, n)
    assert n % tm == 0
    nt = n // tm
    row = lambda i: (i, 0)
    const = lambda i: (0, 0)
    heads = width // HEAD_DIM
    if tail_period is None:
        band_rows, band_map = n, (lambda i: (i, 0, 0))
    else:
        assert nt % tail_period == 0
        band_rows, band_map = (nt // tail_period) * tm, (lambda i: (i // tail_period, 0, 0))
    blk = pl.BlockSpec((tm, width), row)
    blk32 = lambda m: pl.BlockSpec((tm, heads, HEAD_DIM), m)
    out_shape = ([jax.ShapeDtypeStruct((n, width), BF16)] * 6
                 + [jax.ShapeDtypeStruct((n, heads, HEAD_DIM), F32)] * 2
                 + [jax.ShapeDtypeStruct((band_rows, heads, HEAD_DIM), F32)] * 2)
    return pl.pallas_call(
        functools.partial(_norm_proj_kernel, width=width, tail_period=tail_period,
                          sb_qscale=sb_qscale, bd_qscale=bd_qscale),
        out_shape=out_shape,
        grid=(nt,),
        in_specs=[pl.BlockSpec((tm, d), row), pl.BlockSpec((1, d), const),
                  pl.BlockSpec(w_bf16.shape, const)],
        out_specs=[blk] * 6 + [blk32(lambda i: (i, 0, 0))] * 2 + [blk32(band_map)] * 2,
        compiler_params=pltpu.CompilerParams(dimension_semantics=("arbitrary",),
                                             vmem_limit_bytes=VMEM_LIMIT),
        name="norm_proj",
    )(x2, g.reshape(1, d), w_bf16)


def _sb_scores(q, kblk):
    return lax.dot_general(q, kblk, _NT, preferred_element_type=F32)


def _sb_softplus2(z):
    return jnp.maximum(z, jnp.log2(1.0 + jnp.exp2(jnp.minimum(z, SB_EXP_CLAMP))))


def _sb_suffix(sp, tri):
    return jnp.dot(sp.astype(BF16), tri, preferred_element_type=F32)


def _sb_weights(z, sp, excl, carry):
    t = sp + excl if carry is None else sp + excl + carry
    return jnp.exp2(z - t).astype(BF16)


def _sb_block(q, kblk, vblk, tri, carry):
    z = _sb_scores(q, kblk)
    sp = _sb_softplus2(z)
    w = _sb_weights(z, sp, _sb_suffix(sp, tri), carry)
    pv = jnp.dot(w, vblk, preferred_element_type=F32)
    return pv, carry + jnp.sum(sp, axis=-1, keepdims=True)


def _sb_kernel(q_ref, kd_ref, vd_ref, kp_ref, vp_ref, tri_ref, causal_ref, o_ref, acc_ref, *, bq,
               n_sub, n_past_static):
    tri = tri_ref[...]
    head_lanes = _head_masks((bq, LANES))
    subs = range(n_sub)

    def n_past(u):
        return pl.program_id(2) * n_sub + u if n_past_static is None else n_past_static

    qh, kv, no_past = {}, {}, {}
    for u in subs:
        q = q_ref[u * bq:(u + 1) * bq, :]
        for h in range(2):
            qh[u, h] = jnp.where(head_lanes[h], q, jnp.zeros_like(q))
        rows = slice(u * KEY_BLOCK, (u + 1) * KEY_BLOCK)
        kv[u, "diag"] = kd_ref[rows, :].astype(BF16), vd_ref[rows, :].astype(BF16)
        r1 = pl.multiple_of(jnp.maximum(n_past(u) - 1, 0) * KEY_BLOCK, KEY_BLOCK)
        kv[u, "prev"] = (kp_ref[pl.ds(r1, KEY_BLOCK), :].astype(BF16),
                         vp_ref[pl.ds(r1, KEY_BLOCK), :].astype(BF16))
        no_past[u] = jnp.where(n_past(u) > 0, 0.0, -NEG_INF).astype(F32)

    chains = [(u, h, blk) for u in subs for blk in ("diag", "prev") for h in range(2)]
    z = {}
    for u, h, blk in chains:
        z[u, h, blk] = _sb_scores(qh[u, h], kv[u, blk][0])
        if blk == "diag":
            z[u, h, blk] = z[u, h, blk] + causal_ref[...]
    sp, excl = {}, {}
    for c in chains:
        sp[c] = _sb_softplus2(z[c])
        excl[c] = _sb_suffix(sp[c], tri)
    swept = {c: jnp.sum(sp[c], axis=-1, keepdims=True) for c in chains}
    pv = {}
    for u, h, blk in chains:
        carry = None if blk == "diag" else swept[u, h, "diag"] + no_past[u]
        w = _sb_weights(z[u, h, blk], sp[u, h, blk], excl[u, h, blk], carry)
        pv[u, h, blk] = jnp.dot(w, kv[u, blk][1], preferred_element_type=F32)

    def remaining(c0, c1):
        return jnp.min(jnp.minimum(c0, c1))

    def cond(state):
        j, least, _, _ = state
        return jnp.logical_and(j >= 0, least < SB_CUTOFF)

    for u in subs:
        carries = []
        for h in range(2):
            acc_ref[u, h] = pv[u, h, "diag"] + pv[u, h, "prev"]
            carries.append(swept[u, h, "diag"] + no_past[u] + swept[u, h, "prev"])

        def body(state, u=u):
            j, _, c0, c1 = state
            r0 = pl.multiple_of(j * KEY_BLOCK, KEY_BLOCK)
            kblk = kp_ref[pl.ds(r0, KEY_BLOCK), :].astype(BF16)
            vblk = vp_ref[pl.ds(r0, KEY_BLOCK), :].astype(BF16)
            pv0, c0 = _sb_block(qh[u, 0], kblk, vblk, tri, c0)
            pv1, c1 = _sb_block(qh[u, 1], kblk, vblk, tri, c1)
            acc_ref[u, 0] += pv0
            acc_ref[u, 1] += pv1
            return j - 1, remaining(c0, c1), c0, c1

        lax.while_loop(cond, body, (n_past(u) - 2, remaining(*carries), carries[0], carries[1]))
        o_ref[u * bq:(u + 1) * bq, :] = jnp.where(head_lanes[0], acc_ref[u, 0], acc_ref[u, 1])


def _sb_attn(q, kd, vd, kp, vp, *, bq, n_sub, diag_follows_q, n_past_static):
    b, t, w = q.shape
    rows = bq * n_sub
    assert t % rows == 0 and bq <= KEY_BLOCK and w % LANES == 0
    assert n_sub == 1 or bq == KEY_BLOCK
    assert kd.shape[1] % (n_sub * KEY_BLOCK) == 0 and kp.shape[1] % KEY_BLOCK == 0
    row = lax.broadcasted_iota(jnp.int32, (KEY_BLOCK, KEY_BLOCK), 0)
    col = lax.broadcasted_iota(jnp.int32, (KEY_BLOCK, KEY_BLOCK), 1)
    tri = (row > col).astype(BF16)
    causal = jnp.where(col < row, 0.0, NEG_INF).astype(F32)[:bq]
    qmap = lambda bi, hp, i: (bi, i, hp)
    dmap = qmap if diag_follows_q else (lambda bi, hp, i: (bi, 0, hp))
    pmap = lambda bi, hp, i: (bi, 0, hp)
    const = lambda bi, hp, i: (0, 0)
    return pl.pallas_call(
        functools.partial(_sb_kernel, bq=bq, n_sub=n_sub, n_past_static=n_past_static),
        out_shape=jax.ShapeDtypeStruct((b, t, w), F32),
        grid=(b, w // LANES, t // rows),
        in_specs=[pl.BlockSpec((None, rows, LANES), qmap),
                  pl.BlockSpec((None, n_sub * KEY_BLOCK, LANES), dmap),
                  pl.BlockSpec((None, n_sub * KEY_BLOCK, LANES), dmap),
                  pl.BlockSpec((None, kp.shape[1], LANES), pmap),
                  pl.BlockSpec((None, vp.shape[1], LANES), pmap),
                  pl.BlockSpec((KEY_BLOCK, KEY_BLOCK), const),
                  pl.BlockSpec((bq, KEY_BLOCK), const)],
        out_specs=pl.BlockSpec((None, rows, LANES), qmap),
        compiler_params=pltpu.CompilerParams(
            dimension_semantics=("parallel", "parallel", "arbitrary"),
            vmem_limit_bytes=VMEM_LIMIT),
        scratch_shapes=[pltpu.VMEM((n_sub, 2, bq, LANES), F32)],
        name="sb_attn",
    )(q, kd, vd, kp, vp, tri, causal)


def _band_kernel(q_ref, k_ref, v_ref, bias_ref, o_ref, *, bq, n_sub, kw, back):
    nvar = bias_ref.shape[0]
    head_lanes = _head_masks((bq, LANES))
    units = [(u, h) for u in range(n_sub) for h in range(2)]

    def window(u):
        blk = pl.program_id(2) * n_sub + u
        ks = pl.multiple_of(jnp.maximum(blk * bq - back, 0), bq)
        return ks, jnp.minimum(blk, nvar - 1)

    def scores(u, h):
        ks, var = window(u)
        q = q_ref[u * bq:(u + 1) * bq, :]
        qh = jnp.where(head_lanes[h], q, jnp.zeros_like(q))
        kblk = k_ref[pl.ds(ks, kw), :].astype(BF16)
        return lax.dot_general(qh, kblk, _NT, preferred_element_type=F32) + bias_ref[var, h]

    def attend(u, s):
        p = jnp.exp2(s - jnp.max(s, axis=-1, keepdims=True))
        den = jnp.sum(p, axis=-1, keepdims=True)
        vblk = v_ref[pl.ds(window(u)[0], kw), :].astype(BF16)
        return jnp.dot(p.astype(BF16), vblk, preferred_element_type=F32) / den

    ahead = 2
    s_tiles = {c: scores(*units[c]) for c in range(min(ahead, len(units)))}
    outs = {}
    for c, (u, h) in enumerate(units):
        outs[u, h] = attend(u, s_tiles.pop(c))
        if c + ahead < len(units):
            s_tiles[c + ahead] = scores(*units[c + ahead])
        if h == 1:
            o_ref[u * bq:(u + 1) * bq, :] = jnp.where(head_lanes[0], outs.pop((u, 0)),
                                                      outs.pop((u, 1)))


def _band_attn(q, k, v, bias, *, bq, n_sub, kw, back):
    b, t, w = q.shape
    nvar = bias.shape[0]
    rows = bq * n_sub
    assert t % rows == 0 and w % LANES == 0
    return pl.pallas_call(
        functools.partial(_band_kernel, bq=bq, n_sub=n_sub, kw=kw, back=back),
        out_shape=jax.ShapeDtypeStruct((b, t, w), F32),
        grid=(b, w // LANES, t // rows),
        in_specs=[pl.BlockSpec((None, rows, LANES), lambda bi, hp, i: (bi, i, hp)),
                  pl.BlockSpec((None, k.shape[1], LANES), lambda bi, hp, i: (bi, 0, hp)),
                  pl.BlockSpec((None, v.shape[1], LANES), lambda bi, hp, i: (bi, 0, hp)),
                  pl.BlockSpec((nvar, 2, bq, kw), lambda bi, hp, i: (0, hp, 0, 0))],
        out_specs=pl.BlockSpec((None, rows, LANES), lambda bi, hp, i: (bi, i, hp)),
        compiler_params=pltpu.CompilerParams(
            dimension_semantics=("parallel", "parallel", "arbitrary"),
            vmem_limit_bytes=VMEM_LIMIT),
        name="band_attn",
    )(q, k, v, bias)


def _band_bias(rel_bias, offset, valid):
    rows, cols = valid.shape
    w = rows + cols
    m = jnp.arange(w)
    rel = jnp.clip(offset - jnp.where(m < cols, m, m - w), -REL_CLIP, REL_CLIP) + REL_CLIP
    base = rel_bias.astype(F32)[:, rel] * LOG2E
    tiled = jnp.tile(base, (1, rows))[:, :rows * (w - 1)].reshape(-1, rows, w - 1)
    return jnp.where(valid[None], tiled[:, :, :cols], NEG_INF)


def _prompt_band_bias(rel_bias, bq, kw, back):
    tiles = []
    for var in range(back // bq + 1):
        qc = (var * bq + jnp.arange(bq))[:, None] // CHUNK
        kc = jnp.arange(kw)[None, :] // CHUNK
        tiles.append(_band_bias(rel_bias, var * bq, (kc <= qc) & (kc >= qc - PAST_CHUNKS)))
    return jnp.stack(tiles)


def _out_ffn_kernel(h_ref, osb_ref, obd_ref, gsb_ref, gbd_ref, wout_ref, gffn_ref, wup_ref, wdn_ref,
                    gfin_ref, y_ref, *, w_sb, ff_chunk, apply_final):
    ysb = _rms(osb_ref[...], gsb_ref[...]).astype(BF16)
    ybd = _rms(obd_ref[...], gbd_ref[...]).astype(BF16)
    h1 = (h_ref[...]
          + jnp.dot(ysb, wout_ref[:w_sb, :], preferred_element_type=F32)
          + jnp.dot(ybd, wout_ref[w_sb:, :], preferred_element_type=F32))
    xn = _rms(h1, gffn_ref[...]).astype(BF16)
    mlp = None
    for c in range(wup_ref.shape[1] // ff_chunk):
        u = jnp.dot(xn, wup_ref[:, c * ff_chunk:(c + 1) * ff_chunk], preferred_element_type=F32)
        a = jnp.square(jnp.maximum(u, 0.0)).astype(BF16)
        d = jnp.dot(a, wdn_ref[c * ff_chunk:(c + 1) * ff_chunk, :], preferred_element_type=F32)
        mlp = d if mlp is None else mlp + d
    h2 = h1 + mlp
    y_ref[...] = _rms(h2, gfin_ref[...]) if apply_final else h2


def _out_ffn(h2, osb, obd, g_sb, g_bd, w_out, g_ffn, w_up, w_dn, g_fin, *, apply_final):
    n, d = h2.shape
    w_sb, w_bd = osb.shape[1], obd.shape[1]
    tm = min(ROW_TILE, n)
    assert n % tm == 0
    row = lambda i: (i, 0)
    const = lambda i: (0, 0)
    resident = lambda a: pl.BlockSpec(a.shape, const, pipeline_mode=pl.Buffered(1))
    vec = lambda a: a.reshape(1, -1).astype(F32)
    args = (h2, osb, obd, vec(g_sb), vec(g_bd), w_out, vec(g_ffn), w_up, w_dn, vec(g_fin))
    in_specs = [pl.BlockSpec((tm, d), row), pl.BlockSpec((tm, w_sb), row),
                pl.BlockSpec((tm, w_bd), row)] + [resident(a) for a in args[3:]]
    return pl.pallas_call(
        functools.partial(_out_ffn_kernel, w_sb=w_sb, ff_chunk=min(1024, w_up.shape[1]),
                          apply_final=apply_final),
        out_shape=jax.ShapeDtypeStruct((n, d), F32),
        grid=(n // tm,),
        in_specs=in_specs,
        out_specs=pl.BlockSpec((tm, d), row),
        compiler_params=pltpu.CompilerParams(dimension_semantics=("arbitrary",),
                                             vmem_limit_bytes=VMEM_LIMIT),
        name="out_ffn",
    )(*args)


def _pad_rows(a, rows):
    return jnp.pad(a, ((0, 0), (0, rows - a.shape[1]), (0, 0)))


def kernel(x_prompt, x_sample, cache_sb_k, cache_sb_v, cache_band_k, cache_band_v, norm_mix_g, w_in,
           rel_bias, norm_sb_g, norm_band_g, w_out, norm_ffn_g, w_up, w_down, norm_final_g):
    depth = w_in.shape[0]
    bsz, seq, d_model = x_prompt.shape
    dbsz, dseq, _ = x_sample.shape
    past = cache_sb_k.shape[2]
    band_rows = cache_band_k.shape[2]
    n_sb, n_bd = cache_sb_k.shape[3], cache_band_k.shape[3]
    w_sb, w_bd = n_sb * HEAD_DIM, n_bd * HEAD_DIM
    keep_p = min(PAST_CHUNKS * CHUNK, seq)
    back = PAST_CHUNKS * CHUNK
    assert w_sb == w_bd and w_in.shape[2] == 6 * w_sb
    assert seq % KEY_BLOCK == 0 and seq >= back + KEY_BLOCK and keep_p == ROW_TILE
    assert dseq <= KEY_BLOCK and past % KEY_BLOCK == 0 and band_rows == back
    sb_qscale = LOG2E / math.sqrt(HEAD_DIM)
    bd_qscale = LOG2E / math.sqrt(HEAD_DIM)

    h_p, h_s = x_prompt, x_sample
    outs = [[] for _ in range(8)]
    for l in range(depth):
        w_in_l = w_in[l].astype(BF16)
        w_out_l, w_up_l, w_dn_l = (w[l].astype(BF16) for w in (w_out, w_up, w_down))
        last = l == depth - 1
        tail = dict(g_sb=norm_sb_g[l], g_bd=norm_band_g[l], w_out=w_out_l, g_ffn=norm_ffn_g[l],
                    w_up=w_up_l, w_dn=w_dn_l, g_fin=norm_final_g, apply_final=last)

        n = bsz * seq
        (qsb, ksb, vsb, qbd, kbd, vbd, ksb32, vsb32, kbd32, vbd32) = _norm_proj(
            h_p.reshape(n, d_model), norm_mix_g[l], w_in_l, tail_period=seq // ROW_TILE,
            sb_qscale=sb_qscale, bd_qscale=bd_qscale)
        r3 = lambda a: a.reshape(bsz, -1, a.shape[-1])
        qsb, ksb, vsb, qbd, kbd, vbd = map(r3, (qsb, ksb, vsb, qbd, kbd, vbd))
        o_sb = _sb_attn(qsb, ksb, vsb, ksb, vsb, bq=KEY_BLOCK, n_sub=SB_SUB_BLOCKS,
                        diag_follows_q=True, n_past_static=None)
        kw = back + KEY_BLOCK
        o_bd = _band_attn(qbd, kbd, vbd, _prompt_band_bias(rel_bias[l], KEY_BLOCK, kw, back),
                          bq=KEY_BLOCK, n_sub=BAND_SUB_BLOCKS, kw=kw, back=back)
        h_p = _out_ffn(h_p.reshape(n, d_model), o_sb.reshape(n, w_sb), o_bd.reshape(n, w_bd),
                       **tail).reshape(bsz, seq, d_model)
        outs[0].append(ksb32.reshape(bsz, seq, n_sb, HEAD_DIM))
        outs[1].append(vsb32.reshape(bsz, seq, n_sb, HEAD_DIM))
        outs[2].append(kbd32.reshape(bsz, keep_p, n_bd, HEAD_DIM))
        outs[3].append(vbd32.reshape(bsz, keep_p, n_bd, HEAD_DIM))

        n = dbsz * dseq
        (qsb, ksb, vsb, qbd, kbd, vbd, ksb32, vsb32, kbd32, vbd32) = _norm_proj(
            h_s.reshape(n, d_model), norm_mix_g[l], w_in_l, tail_period=None,
            sb_qscale=sb_qscale, bd_qscale=bd_qscale)
        r3 = lambda a: a.reshape(dbsz, -1, a.shape[-1])
        qsb, ksb, vsb, qbd, kbd, vbd = map(r3, (qsb, ksb, vsb, qbd, kbd, vbd))
        o_sb = _sb_attn(qsb, _pad_rows(ksb, KEY_BLOCK), _pad_rows(vsb, KEY_BLOCK),
                        cache_sb_k[l].reshape(dbsz, past, w_sb).astype(BF16),
                        cache_sb_v[l].reshape(dbsz, past, w_sb).astype(BF16),
                        bq=dseq, n_sub=1, diag_follows_q=False, n_past_static=past // KEY_BLOCK)
        kw = -(-(band_rows + dseq) // LANES) * LANES
        kb_all = _pad_rows(jnp.concatenate(
            [cache_band_k[l].reshape(dbsz, band_rows, w_bd).astype(BF16), kbd], axis=1), kw)
        vb_all = _pad_rows(jnp.concatenate(
            [cache_band_v[l].reshape(dbsz, band_rows, w_bd).astype(BF16), vbd], axis=1), kw)
        k_pos = past - band_rows + jnp.arange(kw)
        valid = (k_pos >= 0) & (k_pos < past + dseq)
        bias = _band_bias(rel_bias[l], band_rows, jnp.broadcast_to(valid[None, :], (dseq, kw)))[None]
        o_bd = _band_attn(qbd, kb_all, vb_all, bias, bq=dseq, n_sub=1, kw=kw, back=0)
        h_s = _out_ffn(h_s.reshape(n, d_model), o_sb.reshape(n, w_sb), o_bd.reshape(n, w_bd),
                       **tail).reshape(dbsz, dseq, d_model)
        outs[4].append(ksb32.reshape(dbsz, dseq, n_sb, HEAD_DIM))
        outs[5].append(vsb32.reshape(dbsz, dseq, n_sb, HEAD_DIM))
        outs[6].append(kbd32.reshape(dbsz, dseq, n_bd, HEAD_DIM))
        outs[7].append(vbd32.reshape(dbsz, dseq, n_bd, HEAD_DIM))

    return (h_p, h_s) + tuple(jnp.stack(o) for o in outs)
```

```python
import functools
import math

import jax
import jax.numpy as jnp
from jax import lax
from jax.experimental import pallas as pl
from jax.experimental.pallas import tpu as pltpu

F32 = jnp.float32
BF16 = jnp.bfloat16

HEAD_DIM = 64
CHUNK = 64
PAST_CHUNKS = 8
REL_CLIP = 2 * CHUNK
EPS = 1e-6
NEG_INF = -1e30
LOG2E = 1.4426950408889634
SB_CUTOFF = 150.0
SB_EXP_CLAMP = 64.0

LANES = 128
KEY_BLOCK = 256
ROW_TILE = 512
BAND_SUB_BLOCKS = 4
SB_SUB_BLOCKS = 4
VMEM_LIMIT = 56 * 1024 * 1024

_NT = (((1,), (1,)), ((), ()))


def _rms(x, g):
    return x * lax.rsqrt(jnp.mean(x * x, axis=-1, keepdims=True) + EPS) * g


def _head_masks(shape):
    lane = lax.broadcasted_iota(jnp.int32, shape, 1)
    return lane < HEAD_DIM, lane >= HEAD_DIM


def _merge_heads(blk):
    return blk.reshape(blk.shape[0], -1).astype(BF16)


def _norm_proj_kernel(x_ref, g_ref, w_ref, qsb_ref, ksb_ref, vsb_ref, qbd_ref, kbd_ref, vbd_ref,
                      ksb32_ref, vsb32_ref, kbd32_ref, vbd32_ref, *, width, tail_period, sb_qscale,
                      bd_qscale):
    xn = _rms(x_ref[...], g_ref[...]).astype(BF16)

    def proj(c):
        return jnp.dot(xn, w_ref[:, c * width:(c + 1) * width], preferred_element_type=F32)

    def split_heads(p):
        return p.reshape(p.shape[0], -1, HEAD_DIM)

    qsb_ref[...] = (proj(0) * sb_qscale).astype(BF16)
    p = proj(1)
    ksb_ref[...] = p.astype(BF16)
    ksb32_ref[...] = split_heads(p)
    p = proj(2)
    vsb_ref[...] = p.astype(BF16)
    vsb32_ref[...] = split_heads(p)
    qbd_ref[...] = (proj(3) * bd_qscale).astype(BF16)
    pk = proj(4)
    kbd_ref[...] = pk.astype(BF16)
    pv = proj(5)
    vbd_ref[...] = pv.astype(BF16)

    def write_band32():
        kbd32_ref[...] = split_heads(pk)
        vbd32_ref[...] = split_heads(pv)

    if tail_period is None:
        write_band32()
    else:
        pl.when(pl.program_id(0) % tail_period == tail_period - 1)(write_band32)


def _norm_proj(x2, g, w_bf16, *, tail_period, sb_qscale, bd_qscale):
    n, d = x2.shape
    width = w_bf16.shape[1] // 6
    tm = min(ROW_TILE, n)
    assert n % tm == 0
    nt = n // tm
    row = lambda i: (i, 0)
    const = lambda i: (0, 0)
    heads = width // HEAD_DIM
    if tail_period is None:
        band_rows, band_map = n, (lambda i: (i, 0, 0))
    else:
        assert nt % tail_period == 0
        band_rows, band_map = (nt // tail_period) * tm, (lambda i: (i // tail_period, 0, 0))
    blk = pl.BlockSpec((tm, width), row)
    blk32 = lambda m: pl.BlockSpec((tm, heads, HEAD_DIM), m)
    out_shape = ([jax.ShapeDtypeStruct((n, width), BF16)] * 6
                 + [jax.ShapeDtypeStruct((n, heads, HEAD_DIM), F32)] * 2
                 + [jax.ShapeDtypeStruct((band_rows, heads, HEAD_DIM), F32)] * 2)
    return pl.pallas_call(
        functools.partial(_norm_proj_kernel, width=width, tail_period=tail_period,
                          sb_qscale=sb_qscale, bd_qscale=bd_qscale),
        out_shape=out_shape,
        grid=(nt,),
        in_specs=[pl.BlockSpec((tm, d), row), pl.BlockSpec((1, d), const),
                  pl.BlockSpec(w_bf16.shape, const)],
        out_specs=[blk] * 6 + [blk32(lambda i: (i, 0, 0))] * 2 + [blk32(band_map)] * 2,
        compiler_params=pltpu.CompilerParams(dimension_semantics=("arbitrary",),
                                             vmem_limit_bytes=VMEM_LIMIT),
        name="norm_proj",
    )(x2, g.reshape(1, d), w_bf16)


def _sb_softplus2(z):
    return jnp.maximum(z, jnp.log2(1.0 + jnp.exp2(jnp.minimum(z, SB_EXP_CLAMP))))


def _sb_chains(chains, tri):
    z = []
    for q, k, _, bias, _ in chains:
        s = lax.dot_general(q, k, _NT, preferred_element_type=F32)
        z.append(s if bias is None else s + bias)
    sp, excl = [], []
    for zc in z:
        sp.append(_sb_softplus2(zc))
        excl.append(jnp.dot(sp[-1].astype(BF16), tri, preferred_element_type=F32))
    swept = [jnp.sum(s, axis=-1, keepdims=True) for s in sp]
    pv = []
    for (_, _, v, _, carry_fn), zc, s, e in zip(chains, z, sp, excl):
        carry = carry_fn(swept)
        t = s + e if carry is None else s + e + carry
        w = jnp.exp2(zc - t).astype(BF16)
        pv.append(jnp.dot(w, v, preferred_element_type=F32))
    return pv, swept


def _sb_sweep(qs, diag_kv, n_past, past_kv, tri, causal, o_ref, out_idx):
    units = range(len(qs))
    head_lanes = _head_masks(qs[0].shape)
    qh = [[jnp.where(m, q, jnp.zeros_like(q)) for m in head_lanes] for q in qs]
    big = lambda ok: jnp.where(ok, 0.0, -NEG_INF).astype(F32)

    memo, chains, no_past = {}, [], []
    for u in units:
        no_past.append(big(n_past[u] > 0))
        kp, vp = past_kv(u, jnp.maximum(n_past[u] - 1, 0), memo)
        for h in range(2):
            d = len(chains)
            chains.append((qh[u][h], *diag_kv[u], causal, lambda swept: None))
            chains.append((qh[u][h], kp, vp, None,
                           lambda swept, d=d, u=u: swept[d] + no_past[u]))
    pv, swept = _sb_chains(chains, tri)
    carries = []
    for u in units:
        c = 4 * u
        o_ref[out_idx[u]] = jnp.where(head_lanes[0], pv[c] + pv[c + 1], pv[c + 2] + pv[c + 3])
        carries += [swept[c] + no_past[u] + swept[c + 1], swept[c + 2] + no_past[u] + swept[c + 3]]

    def least(cs):
        return jnp.min(functools.reduce(jnp.minimum, cs))

    most_past = functools.reduce(jnp.maximum, n_past)

    def cond(state):
        return jnp.logical_and(state[0] < most_past - 1, state[1] < SB_CUTOFF)

    def body(state):
        t, cs = state[0], state[2:]
        memo, chains, start = {}, [], []
        for u in units:
            j = n_past[u] - 2 - t
            k, v = past_kv(u, jnp.maximum(j, 0), memo)
            for h in range(2):
                start.append(cs[2 * u + h] + big(j >= 0))
                chains.append((qh[u][h], k, v, None, lambda swept, c=start[-1]: c))
        pv, swept = _sb_chains(chains, tri)
        for u in units:
            o_ref[out_idx[u]] += jnp.where(head_lanes[0], pv[2 * u], pv[2 * u + 1])
        new = [c + s for c, s in zip(start, swept)]
        return (t + 1, least(new), *new)

    lax.while_loop(cond, body, (0, least(carries), *carries))


def _sb_prompt_kernel(q_ref, kd_ref, vd_ref, kp_ref, vp_ref, tri_ref, causal_ref, o_ref, *, n_sub):
    rows = lambda u: slice(u * KEY_BLOCK, (u + 1) * KEY_BLOCK)

    def past_kv(u, j, memo):
        r0 = pl.multiple_of(j * KEY_BLOCK, KEY_BLOCK)
        return kp_ref[pl.ds(r0, KEY_BLOCK), :], vp_ref[pl.ds(r0, KEY_BLOCK), :]

    _sb_sweep(qs=[q_ref[rows(u), :] for u in range(n_sub)],
              diag_kv=[(kd_ref[rows(u), :], vd_ref[rows(u), :]) for u in range(n_sub)],
              n_past=[pl.program_id(2) * n_sub + u for u in range(n_sub)],
              past_kv=past_kv, tri=tri_ref[...], causal=causal_ref[...], o_ref=o_ref,
              out_idx=[(rows(u), slice(None)) for u in range(n_sub)])


def _sb_sample_kernel(q_ref, kn_ref, vn_ref, ck_ref, cv_ref, tri_ref, causal_ref, o_ref, *, n_past):
    pairs = q_ref.shape[1] // LANES
    lanes = lambda hp: slice(hp * LANES, (hp + 1) * LANES)

    def past_kv(hp, j, memo):
        if "blk" not in memo:
            r0 = pl.multiple_of(j * KEY_BLOCK, KEY_BLOCK)
            memo["blk"] = (_merge_heads(ck_ref[pl.ds(r0, KEY_BLOCK), :, :]),
                           _merge_heads(cv_ref[pl.ds(r0, KEY_BLOCK), :, :]))
        k, v = memo["blk"]
        return k[:, lanes(hp)], v[:, lanes(hp)]

    _sb_sweep(qs=[q_ref[:, lanes(hp)] for hp in range(pairs)],
              diag_kv=[(kn_ref[:, lanes(hp)], vn_ref[:, lanes(hp)]) for hp in range(pairs)],
              n_past=[n_past] * pairs, past_kv=past_kv, tri=tri_ref[...], causal=causal_ref[...],
              o_ref=o_ref, out_idx=[(slice(None), lanes(hp)) for hp in range(pairs)])


def _sb_constants(bq):
    row = lax.broadcasted_iota(jnp.int32, (KEY_BLOCK, KEY_BLOCK), 0)
    col = lax.broadcasted_iota(jnp.int32, (KEY_BLOCK, KEY_BLOCK), 1)
    tri = (row > col).astype(BF16)
    causal = jnp.where(col < row, 0.0, NEG_INF).astype(F32)[:bq]
    return tri, causal


def _sb_attn_prompt(q, k, v, *, n_sub):
    b, t, w = q.shape
    rows = KEY_BLOCK * n_sub
    assert t % rows == 0 and w % LANES == 0
    tri, causal = _sb_constants(KEY_BLOCK)
    qmap = lambda bi, hp, i: (bi, i, hp)
    pmap = lambda bi, hp, i: (bi, 0, hp)
    const = lambda bi, hp, i: (0, 0)
    return pl.pallas_call(
        functools.partial(_sb_prompt_kernel, n_sub=n_sub),
        out_shape=jax.ShapeDtypeStruct((b, t, w), F32),
        grid=(b, w // LANES, t // rows),
        in_specs=[pl.BlockSpec((None, rows, LANES), qmap),
                  pl.BlockSpec((None, rows, LANES), qmap),
                  pl.BlockSpec((None, rows, LANES), qmap),
                  pl.BlockSpec((None, t, LANES), pmap),
                  pl.BlockSpec((None, t, LANES), pmap),
                  pl.BlockSpec((KEY_BLOCK, KEY_BLOCK), const),
                  pl.BlockSpec((KEY_BLOCK, KEY_BLOCK), const)],
        out_specs=pl.BlockSpec((None, rows, LANES), qmap),
        compiler_params=pltpu.CompilerParams(
            dimension_semantics=("parallel", "parallel", "arbitrary"),
            vmem_limit_bytes=VMEM_LIMIT),
        name="sb_attn",
    )(q, k, v, k, v, tri, causal)


def _sb_attn_sample(q, k_new, v_new, cache_k, cache_v, layer):
    b, t, w = q.shape
    past, heads = cache_k.shape[2], cache_k.shape[3]
    assert t <= KEY_BLOCK and past % KEY_BLOCK == 0 and heads * HEAD_DIM == w and w % LANES == 0
    tri, causal = _sb_constants(t)
    pad = lambda a: jnp.pad(a, ((0, 0), (0, KEY_BLOCK - t), (0, 0)))
    tok = lambda rows: pl.BlockSpec((None, rows, w), lambda bi: (bi, 0, 0))
    cache = pl.BlockSpec((None, None, past, heads, HEAD_DIM), lambda bi: (layer, bi, 0, 0, 0))
    const = lambda bi: (0, 0)
    return pl.pallas_call(
        functools.partial(_sb_sample_kernel, n_past=past // KEY_BLOCK),
        out_shape=jax.ShapeDtypeStruct((b, t, w), F32),
        grid=(b,),
        in_specs=[tok(t), tok(KEY_BLOCK), tok(KEY_BLOCK), cache, cache,
                  pl.BlockSpec((KEY_BLOCK, KEY_BLOCK), const), pl.BlockSpec((t, KEY_BLOCK), const)],
        out_specs=tok(t),
        compiler_params=pltpu.CompilerParams(dimension_semantics=("arbitrary",),
                                             vmem_limit_bytes=VMEM_LIMIT),
        name="sb_attn_sample",
    )(q, pad(k_new), pad(v_new), cache_k, cache_v, tri, causal)


def _band_units(units):
    ahead = 2

    def scores(unit):
        q, k, _, bias = unit
        return lax.dot_general(q(), k(), _NT, preferred_element_type=F32) + bias()

    def attend(unit, s):
        p = jnp.exp2(s - jnp.max(s, axis=-1, keepdims=True))
        den = jnp.sum(p, axis=-1, keepdims=True)
        return jnp.dot(p.astype(BF16), unit[2](), preferred_element_type=F32) / den

    s_tiles = {c: scores(units[c]) for c in range(min(ahead, len(units)))}
    outs = []
    for c, unit in enumerate(units):
        outs.append(attend(unit, s_tiles.pop(c)))
        if c + ahead < len(units):
            s_tiles[c + ahead] = scores(units[c + ahead])
    return outs


def _band_prompt_kernel(q_ref, k_ref, v_ref, bias_ref, o_ref, *, bq, n_sub, kw, back):
    nvar = bias_ref.shape[0]
    head_lanes = _head_masks((bq, LANES))

    def unit(u, h):
        blk = pl.program_id(2) * n_sub + u
        ks = pl.multiple_of(jnp.maximum(blk * bq - back, 0), bq)
        var = jnp.minimum(blk, nvar - 1)

        def q():
            qb = q_ref[u * bq:(u + 1) * bq, :]
            return jnp.where(head_lanes[h], qb, jnp.zeros_like(qb))

        return (q, lambda: k_ref[pl.ds(ks, kw), :], lambda: v_ref[pl.ds(ks, kw), :],
                lambda: bias_ref[var, h])

    outs = _band_units([unit(u, h) for u in range(n_sub) for h in range(2)])
    for u in range(n_sub):
        o_ref[u * bq:(u + 1) * bq, :] = jnp.where(head_lanes[0], outs[2 * u], outs[2 * u + 1])


def _band_sample_kernel(q_ref, kn_ref, vn_ref, ck_ref, cv_ref, bias_ref, o_ref):
    pairs = q_ref.shape[1] // LANES
    lanes = lambda hp: slice(hp * LANES, (hp + 1) * LANES)
    head_lanes = _head_masks((q_ref.shape[0], LANES))
    kwin = jnp.concatenate([_merge_heads(ck_ref[...]), kn_ref[...]], axis=0)
    vwin = jnp.concatenate([_merge_heads(cv_ref[...]), vn_ref[...]], axis=0)

    def unit(hp, h):
        def q():
            qb = q_ref[:, lanes(hp)]
            return jnp.where(head_lanes[h], qb, jnp.zeros_like(qb))

        return (q, lambda: kwin[:, lanes(hp)], lambda: vwin[:, lanes(hp)],
                lambda: bias_ref[2 * hp + h])

    outs = _band_units([unit(hp, h) for hp in range(pairs) for h in range(2)])
    for hp in range(pairs):
        o_ref[:, lanes(hp)] = jnp.where(head_lanes[0], outs[2 * hp], outs[2 * hp + 1])


def _band_attn_prompt(q, k, v, bias, *, bq, n_sub, kw, back):
    b, t, w = q.shape
    nvar = bias.shape[0]
    rows = bq * n_sub
    assert t % rows == 0 and w % LANES == 0
    return pl.pallas_call(
        functools.partial(_band_prompt_kernel, bq=bq, n_sub=n_sub, kw=kw, back=back),
        out_shape=jax.ShapeDtypeStruct((b, t, w), F32),
        grid=(b, w // LANES, t // rows),
        in_specs=[pl.BlockSpec((None, rows, LANES), lambda bi, hp, i: (bi, i, hp)),
                  pl.BlockSpec((None, t, LANES), lambda bi, hp, i: (bi, 0, hp)),
                  pl.BlockSpec((None, t, LANES), lambda bi, hp, i: (bi, 0, hp)),
                  pl.BlockSpec((nvar, 2, bq, kw), lambda bi, hp, i: (0, hp, 0, 0))],
        out_specs=pl.BlockSpec((None, rows, LANES), lambda bi, hp, i: (bi, i, hp)),
        compiler_params=pltpu.CompilerParams(
            dimension_semantics=("parallel", "parallel", "arbitrary"),
            vmem_limit_bytes=VMEM_LIMIT),
        name="band_attn",
    )(q, k, v, bias)


def _band_attn_sample(q, k_new, v_new, cache_k, cache_v, bias, layer):
    b, t, w = q.shape
    tn = k_new.shape[1]
    rows, heads = cache_k.shape[2], cache_k.shape[3]
    assert heads * HEAD_DIM == w and w % LANES == 0 and bias.shape == (heads, t, rows + tn)
    tok = lambda r: pl.BlockSpec((None, r, w), lambda bi: (bi, 0, 0))
    cache = pl.BlockSpec((None, None, rows, heads, HEAD_DIM), lambda bi: (layer, bi, 0, 0, 0))
    return pl.pallas_call(
        _band_sample_kernel,
        out_shape=jax.ShapeDtypeStruct((b, t, w), F32),
        grid=(b,),
        in_specs=[tok(t), tok(tn), tok(tn), cache, cache,
                  pl.BlockSpec(bias.shape, lambda bi: (0, 0, 0))],
        out_specs=tok(t),
        compiler_params=pltpu.CompilerParams(dimension_semantics=("arbitrary",),
                                             vmem_limit_bytes=VMEM_LIMIT),
        name="band_attn_sample",
    )(q, k_new, v_new, cache_k, cache_v, bias)


def _band_bias(rel_bias, offset, valid):
    rows, cols = valid.shape
    w = rows + cols
    m = jnp.arange(w)
    rel = jnp.clip(offset - jnp.where(m < cols, m, m - w), -REL_CLIP, REL_CLIP) + REL_CLIP
    base = rel_bias.astype(F32)[:, rel] * LOG2E
    tiled = jnp.tile(base, (1, rows))[:, :rows * (w - 1)].reshape(-1, rows, w - 1)
    return jnp.where(valid[None], tiled[:, :, :cols], NEG_INF)


def _prompt_band_bias(rel_bias, bq, kw, back):
    tiles = []
    for var in range(back // bq + 1):
        qc = (var * bq + jnp.arange(bq))[:, None] // CHUNK
        kc = jnp.arange(kw)[None, :] // CHUNK
        tiles.append(_band_bias(rel_bias, var * bq, (kc <= qc) & (kc >= qc - PAST_CHUNKS)))
    return jnp.stack(tiles)


def _out_ffn_kernel(h_ref, osb_ref, obd_ref, gsb_ref, gbd_ref, wout_ref, gffn_ref, wup_ref, wdn_ref,
                    gfin_ref, y_ref, *, w_sb, ff_chunk, apply_final):
    ysb = _rms(osb_ref[...], gsb_ref[...]).astype(BF16)
    ybd = _rms(obd_ref[...], gbd_ref[...]).astype(BF16)
    h1 = (h_ref[...]
          + jnp.dot(ysb, wout_ref[:w_sb, :], preferred_element_type=F32)
          + jnp.dot(ybd, wout_ref[w_sb:, :], preferred_element_type=F32))
    xn = _rms(h1, gffn_ref[...]).astype(BF16)
    mlp = None
    for c in range(wup_ref.shape[1] // ff_chunk):
        u = jnp.dot(xn, wup_ref[:, c * ff_chunk:(c + 1) * ff_chunk], preferred_element_type=F32)
        a = jnp.square(jnp.maximum(u, 0.0)).astype(BF16)
        d = jnp.dot(a, wdn_ref[c * ff_chunk:(c + 1) * ff_chunk, :], preferred_element_type=F32)
        mlp = d if mlp is None else mlp + d
    h2 = h1 + mlp
    y_ref[...] = _rms(h2, gfin_ref[...]) if apply_final else h2


def _out_ffn(h2, osb, obd, g_sb, g_bd, w_out, g_ffn, w_up, w_dn, g_fin, *, apply_final):
    n, d = h2.shape
    w_sb, w_bd = osb.shape[1], obd.shape[1]
    tm = min(ROW_TILE, n)
    assert n % tm == 0
    row = lambda i: (i, 0)
    const = lambda i: (0, 0)
    resident = lambda a: pl.BlockSpec(a.shape, const, pipeline_mode=pl.Buffered(1))
    vec = lambda a: a.reshape(1, -1).astype(F32)
    args = (h2, osb, obd, vec(g_sb), vec(g_bd), w_out, vec(g_ffn), w_up, w_dn, vec(g_fin))
    in_specs = [pl.BlockSpec((tm, d), row), pl.BlockSpec((tm, w_sb), row),
                pl.BlockSpec((tm, w_bd), row)] + [resident(a) for a in args[3:]]
    return pl.pallas_call(
        functools.partial(_out_ffn_kernel, w_sb=w_sb, ff_chunk=min(1024, w_up.shape[1]),
                          apply_final=apply_final),
        out_shape=jax.ShapeDtypeStruct((n, d), F32),
        grid=(n // tm,),
        in_specs=in_specs,
        out_specs=pl.BlockSpec((tm, d), row),
        compiler_params=pltpu.CompilerParams(dimension_semantics=("arbitrary",),
                                             vmem_limit_bytes=VMEM_LIMIT),
        name="out_ffn",
    )(*args)


def kernel(x_prompt, x_sample, cache_sb_k, cache_sb_v, cache_band_k, cache_band_v, norm_mix_g, w_in,
           rel_bias, norm_sb_g, norm_band_g, w_out, norm_ffn_g, w_up, w_down, norm_final_g):
    depth = w_in.shape[0]
    bsz, seq, d_model = x_prompt.shape
    dbsz, dseq, _ = x_sample.shape
    past = cache_sb_k.shape[2]
    band_rows = cache_band_k.shape[2]
    n_sb, n_bd = cache_sb_k.shape[3], cache_band_k.shape[3]
    w_sb, w_bd = n_sb * HEAD_DIM, n_bd * HEAD_DIM
    keep_p = min(PAST_CHUNKS * CHUNK, seq)
    back = PAST_CHUNKS * CHUNK
    assert w_sb == w_bd and w_in.shape[2] == 6 * w_sb
    assert seq % KEY_BLOCK == 0 and seq >= back + KEY_BLOCK and keep_p == ROW_TILE
    assert band_rows == back and past >= band_rows
    sb_qscale = LOG2E / math.sqrt(HEAD_DIM)
    bd_qscale = LOG2E / math.sqrt(HEAD_DIM)

    h_p, h_s = x_prompt, x_sample
    outs = [[] for _ in range(8)]
    for l in range(depth):
        w_in_l = w_in[l].astype(BF16)
        w_out_l, w_up_l, w_dn_l = (w[l].astype(BF16) for w in (w_out, w_up, w_down))
        last = l == depth - 1
        tail = dict(g_sb=norm_sb_g[l], g_bd=norm_band_g[l], w_out=w_out_l, g_ffn=norm_ffn_g[l],
                    w_up=w_up_l, w_dn=w_dn_l, g_fin=norm_final_g, apply_final=last)

        n = bsz * seq
        (qsb, ksb, vsb, qbd, kbd, vbd, ksb32, vsb32, kbd32, vbd32) = _norm_proj(
            h_p.reshape(n, d_model), norm_mix_g[l], w_in_l, tail_period=seq // ROW_TILE,
            sb_qscale=sb_qscale, bd_qscale=bd_qscale)
        r3 = lambda a: a.reshape(bsz, -1, a.shape[-1])
        qsb, ksb, vsb, qbd, kbd, vbd = map(r3, (qsb, ksb, vsb, qbd, kbd, vbd))
        o_sb = _sb_attn_prompt(qsb, ksb, vsb, n_sub=SB_SUB_BLOCKS)
        kw = back + KEY_BLOCK
        o_bd = _band_attn_prompt(qbd, kbd, vbd, _prompt_band_bias(rel_bias[l], KEY_BLOCK, kw, back),
                                 bq=KEY_BLOCK, n_sub=BAND_SUB_BLOCKS, kw=kw, back=back)
        h_p = _out_ffn(h_p.reshape(n, d_model), o_sb.reshape(n, w_sb), o_bd.reshape(n, w_bd),
                       **tail).reshape(bsz, seq, d_model)
        outs[0].append(ksb32.reshape(bsz, seq, n_sb, HEAD_DIM))
        outs[1].append(vsb32.reshape(bsz, seq, n_sb, HEAD_DIM))
        outs[2].append(kbd32.reshape(bsz, keep_p, n_bd, HEAD_DIM))
        outs[3].append(vbd32.reshape(bsz, keep_p, n_bd, HEAD_DIM))

        n = dbsz * dseq
        (qsb, ksb, vsb, qbd, kbd, vbd, ksb32, vsb32, kbd32, vbd32) = _norm_proj(
            h_s.reshape(n, d_model), norm_mix_g[l], w_in_l, tail_period=None,
            sb_qscale=sb_qscale, bd_qscale=bd_qscale)
        r3 = lambda a: a.reshape(dbsz, -1, a.shape[-1])
        qsb, ksb, vsb, qbd, kbd, vbd = map(r3, (qsb, ksb, vsb, qbd, kbd, vbd))
        o_sb = _sb_attn_sample(qsb, ksb, vsb, cache_sb_k, cache_sb_v, l)
        new_rows = -(-dseq // LANES) * LANES
        k_pos = past - band_rows + jnp.arange(band_rows + new_rows)
        valid = (k_pos >= 0) & (k_pos < past + dseq)
        bias = _band_bias(rel_bias[l], band_rows, jnp.broadcast_to(valid[None, :], (dseq, valid.size)))
        pad = lambda a: jnp.pad(a, ((0, 0), (0, new_rows - dseq), (0, 0)))
        o_bd = _band_attn_sample(qbd, pad(kbd), pad(vbd), cache_band_k, cache_band_v, bias, l)
        h_s = _out_ffn(h_s.reshape(n, d_model), o_sb.reshape(n, w_sb), o_bd.reshape(n, w_bd),
                       **tail).reshape(dbsz, dseq, d_model)
        outs[4].append(ksb32.reshape(dbsz, dseq, n_sb, HEAD_DIM))
        outs[5].append(vsb32.reshape(dbsz, dseq, n_sb, HEAD_DIM))
        outs[6].append(kbd32.reshape(dbsz, dseq, n_bd, HEAD_DIM))
        outs[7].append(vbd32.reshape(dbsz, dseq, n_bd, HEAD_DIM))

    return (h_p, h_s) + tuple(jnp.stack(o) for o in outs)
```

```python
import functools
import math

import jax
import jax.numpy as jnp
from jax import lax
from jax.experimental import pallas as pl
from jax.experimental.pallas import tpu as pltpu

F32 = jnp.float32
BF16 = jnp.bfloat16

HEAD_DIM = 64
CHUNK = 64
PAST_CHUNKS = 8
REL_CLIP = 2 * CHUNK
EPS = 1e-6
NEG_INF = -1e30
LOG2E = 1.4426950408889634
SB_CUTOFF = 150.0
SB_EXP_CLAMP = 64.0

LANES = 128
KEY_BLOCK = 256
ROW_TILE = 512
BAND_SUB_BLOCKS = 4
SB_SUB_BLOCKS = 4
VMEM_LIMIT = 56 * 1024 * 1024

_NT = (((1,), (1,)), ((), ()))


def _rms(x, g):
    return x * lax.rsqrt(jnp.mean(x * x, axis=-1, keepdims=True) + EPS) * g


def _head_masks(shape):
    lane = lax.broadcasted_iota(jnp.int32, shape, 1)
    return lane < HEAD_DIM, lane >= HEAD_DIM


def _nt(a, b):
    return lax.dot_general(a, b, _NT, preferred_element_type=F32)


def _nn(a, b):
    return jnp.dot(a, b, preferred_element_type=F32)


def _norm_proj_kernel(x_ref, g_ref, w_ref, qsb_ref, ksb_ref, vsb_ref, qbd_ref, kbd_ref, vbd_ref,
                      ksb32_ref, vsb32_ref, kbd32_ref, vbd32_ref, *, width, tail_period, sb_qscale,
                      bd_qscale):
    xn = _rms(x_ref[...], g_ref[...]).astype(BF16)

    def proj(c):
        return _nn(xn, w_ref[:, c * width:(c + 1) * width])

    def split_heads(p):
        return p.reshape(p.shape[0], -1, HEAD_DIM)

    qsb_ref[...] = (proj(0) * sb_qscale).astype(BF16)
    p = proj(1)
    ksb_ref[...] = p.astype(BF16)
    ksb32_ref[...] = split_heads(p)
    p = proj(2)
    vsb_ref[...] = p.astype(BF16)
    vsb32_ref[...] = split_heads(p)
    qbd_ref[...] = (proj(3) * bd_qscale).astype(BF16)
    pk = proj(4)
    kbd_ref[...] = pk.astype(BF16)
    pv = proj(5)
    vbd_ref[...] = pv.astype(BF16)

    def write_band32():
        kbd32_ref[...] = split_heads(pk)
        vbd32_ref[...] = split_heads(pv)

    if tail_period is None:
        write_band32()
    else:
        pl.when(pl.program_id(0) % tail_period == tail_period - 1)(write_band32)


def _norm_proj(x2, g, w_bf16, *, tail_period, sb_qscale, bd_qscale):
    n, d = x2.shape
    width = w_bf16.shape[1] // 6
    tm = min(ROW_TILE, n)
    assert n % tm == 0
    nt = n // tm
    row = lambda i: (i, 0)
    const = lambda i: (0, 0)
    heads = width // HEAD_DIM
    if tail_period is None:
        band_rows, band_map = n, (lambda i: (i, 0, 0))
    else:
        assert nt % tail_period == 0
        band_rows, band_map = (nt // tail_period) * tm, (lambda i: (i // tail_period, 0, 0))
    blk = pl.BlockSpec((tm, width), row)
    blk32 = lambda m: pl.BlockSpec((tm, heads, HEAD_DIM), m)
    out_shape = ([jax.ShapeDtypeStruct((n, width), BF16)] * 6
                 + [jax.ShapeDtypeStruct((n, heads, HEAD_DIM), F32)] * 2
                 + [jax.ShapeDtypeStruct((band_rows, heads, HEAD_DIM), F32)] * 2)
    return pl.pallas_call(
        functools.partial(_norm_proj_kernel, width=width, tail_period=tail_period,
                          sb_qscale=sb_qscale, bd_qscale=bd_qscale),
        out_shape=out_shape,
        grid=(nt,),
        in_specs=[pl.BlockSpec((tm, d), row), pl.BlockSpec((1, d), const),
                  pl.BlockSpec(w_bf16.shape, const)],
        out_specs=[blk] * 6 + [blk32(lambda i: (i, 0, 0))] * 2 + [blk32(band_map)] * 2,
        compiler_params=pltpu.CompilerParams(dimension_semantics=("arbitrary",),
                                             vmem_limit_bytes=VMEM_LIMIT),
        name="norm_proj",
    )(x2, g.reshape(1, d), w_bf16)


def _sb_softplus2(z):
    return jnp.maximum(z, jnp.log2(1.0 + jnp.exp2(jnp.minimum(z, SB_EXP_CLAMP))))


def _sb_chains(chains, tri):
    z = []
    for score_fn, _, bias, _ in chains:
        s = score_fn()
        z.append(s if bias is None else s + bias)
    sp, excl = [], []
    for zc in z:
        sp.append(_sb_softplus2(zc))
        excl.append(_nn(sp[-1].astype(BF16), tri))
    swept = [jnp.sum(s, axis=-1, keepdims=True) for s in sp]
    pv = []
    for (_, pv_fn, _, carry_fn), zc, s, e in zip(chains, z, sp, excl):
        carry = carry_fn(swept)
        t = s + e if carry is None else s + e + carry
        pv.append(pv_fn(jnp.exp2(zc - t).astype(BF16)))
    return pv, swept


def _sb_sweep(streams, n_past, tri, causal, store):
    ids = range(len(streams))
    big = lambda ok: jnp.where(ok, 0.0, -NEG_INF).astype(F32)

    chains = []
    no_past = [big(n > 0) for n in n_past]
    for s in ids:
        diag, past = streams[s]
        chains.append((*diag, causal, lambda swept: None))
        chains.append((*past(jnp.maximum(n_past[s] - 1, 0)), None,
                       lambda swept, s=s: swept[2 * s] + no_past[s]))
    pv, swept = _sb_chains(chains, tri)
    store([pv[2 * s] + pv[2 * s + 1] for s in ids], False)
    carries = [swept[2 * s] + no_past[s] + swept[2 * s + 1] for s in ids]

    def least(cs):
        return jnp.min(functools.reduce(jnp.minimum, cs))

    most_past = functools.reduce(jnp.maximum, n_past)

    def cond(state):
        return jnp.logical_and(state[0] < most_past - 1, state[1] < SB_CUTOFF)

    def body(state):
        t, cs = state[0], state[2:]
        chains, start = [], []
        for s in ids:
            j = n_past[s] - 2 - t
            start.append(cs[s] + big(j >= 0))
            chains.append((*streams[s][1](jnp.maximum(j, 0)), None, lambda swept, c=start[-1]: c))
        pv, swept = _sb_chains(chains, tri)
        store(pv, True)
        new = [c + w for c, w in zip(start, swept)]
        return (t + 1, least(new), *new)

    lax.while_loop(cond, body, (0, least(carries), *carries))


def _sb_prompt_kernel(q_ref, kd_ref, vd_ref, kp_ref, vp_ref, tri_ref, causal_ref, o_ref, *, n_sub):
    rows = lambda u: slice(u * KEY_BLOCK, (u + 1) * KEY_BLOCK)
    head_lanes = _head_masks((KEY_BLOCK, LANES))
    streams, n_past = [], []
    for u in range(n_sub):
        q = q_ref[rows(u), :]
        for m in head_lanes:
            qh = jnp.where(m, q, jnp.zeros_like(q))

            def past(j, qh=qh):
                blk = pl.ds(pl.multiple_of(j * KEY_BLOCK, KEY_BLOCK), KEY_BLOCK)
                return (lambda: _nt(qh, kp_ref[blk, :])), (lambda w: _nn(w, vp_ref[blk, :]))

            diag = ((lambda qh=qh, u=u: _nt(qh, kd_ref[rows(u), :])),
                    (lambda w, u=u: _nn(w, vd_ref[rows(u), :])))
            streams.append((diag, past))
            n_past.append(pl.program_id(2) * n_sub + u)

    def store(pv, accumulate):
        for u in range(n_sub):
            val = jnp.where(head_lanes[0], pv[2 * u], pv[2 * u + 1])
            o_ref[rows(u), :] = o_ref[rows(u), :] + val if accumulate else val

    _sb_sweep(streams, n_past, tri_ref[...], causal_ref[...], store)


def _sb_sample_kernel(q_ref, kn_ref, vn_ref, ck_ref, cv_ref, tri_ref, causal_ref, o_ref, *, n_past):
    heads = ck_ref.shape[0]
    streams = []
    for h in range(heads):
        hl = slice(h * HEAD_DIM, (h + 1) * HEAD_DIM)
        qh = q_ref[:, hl]

        def past(j, h=h, qh=qh):
            blk = pl.ds(pl.multiple_of(j * KEY_BLOCK, KEY_BLOCK), KEY_BLOCK)
            return ((lambda: _nn(qh, ck_ref[h, :, blk].astype(BF16))),
                    (lambda w: _nt(w, cv_ref[h, :, blk].astype(BF16))))

        diag = ((lambda qh=qh, hl=hl: _nt(qh, kn_ref[:, hl])), (lambda w, hl=hl: _nn(w, vn_ref[:, hl])))
        streams.append((diag, past))

    def store(pv, accumulate):
        for h in range(heads):
            hl = slice(h * HEAD_DIM, (h + 1) * HEAD_DIM)
            o_ref[:, hl] = o_ref[:, hl] + pv[h] if accumulate else pv[h]

    _sb_sweep(streams, [n_past] * heads, tri_ref[...], causal_ref[...], store)


def _sb_constants(bq):
    row = lax.broadcasted_iota(jnp.int32, (KEY_BLOCK, KEY_BLOCK), 0)
    col = lax.broadcasted_iota(jnp.int32, (KEY_BLOCK, KEY_BLOCK), 1)
    tri = (row > col).astype(BF16)
    causal = jnp.where(col < row, 0.0, NEG_INF).astype(F32)[:bq]
    return tri, causal


def _sb_attn_prompt(q, k, v, *, n_sub):
    b, t, w = q.shape
    rows = KEY_BLOCK * n_sub
    assert t % rows == 0 and w % LANES == 0
    tri, causal = _sb_constants(KEY_BLOCK)
    qmap = lambda bi, hp, i: (bi, i, hp)
    pmap = lambda bi, hp, i: (bi, 0, hp)
    const = lambda bi, hp, i: (0, 0)
    return pl.pallas_call(
        functools.partial(_sb_prompt_kernel, n_sub=n_sub),
        out_shape=jax.ShapeDtypeStruct((b, t, w), F32),
        grid=(b, w // LANES, t // rows),
        in_specs=[pl.BlockSpec((None, rows, LANES), qmap),
                  pl.BlockSpec((None, rows, LANES), qmap),
                  pl.BlockSpec((None, rows, LANES), qmap),
                  pl.BlockSpec((None, t, LANES), pmap),
                  pl.BlockSpec((None, t, LANES), pmap),
                  pl.BlockSpec((KEY_BLOCK, KEY_BLOCK), const),
                  pl.BlockSpec((KEY_BLOCK, KEY_BLOCK), const)],
        out_specs=pl.BlockSpec((None, rows, LANES), qmap),
        compiler_params=pltpu.CompilerParams(
            dimension_semantics=("parallel", "parallel", "arbitrary"),
            vmem_limit_bytes=VMEM_LIMIT),
        name="sb_attn",
    )(q, k, v, k, v, tri, causal)


def _cache_spec(cache_t, layer):
    return pl.BlockSpec((None, None) + cache_t.shape[2:], lambda bi: (layer, bi, 0, 0, 0))


def _sb_attn_sample(q, k_new, v_new, cache_kt, cache_vt, layer):
    b, t, w = q.shape
    heads, past = cache_kt.shape[2], cache_kt.shape[4]
    assert t <= KEY_BLOCK and past % KEY_BLOCK == 0 and heads * HEAD_DIM == w
    tri, causal = _sb_constants(t)
    pad = lambda a: jnp.pad(a, ((0, 0), (0, KEY_BLOCK - t), (0, 0)))
    tok = lambda rows: pl.BlockSpec((None, rows, w), lambda bi: (bi, 0, 0))
    const = lambda bi: (0, 0)
    return pl.pallas_call(
        functools.partial(_sb_sample_kernel, n_past=past // KEY_BLOCK),
        out_shape=jax.ShapeDtypeStruct((b, t, w), F32),
        grid=(b,),
        in_specs=[tok(t), tok(KEY_BLOCK), tok(KEY_BLOCK), _cache_spec(cache_kt, layer),
                  _cache_spec(cache_vt, layer), pl.BlockSpec((KEY_BLOCK, KEY_BLOCK), const),
                  pl.BlockSpec((t, KEY_BLOCK), const)],
        out_specs=tok(t),
        compiler_params=pltpu.CompilerParams(dimension_semantics=("arbitrary",),
                                             vmem_limit_bytes=VMEM_LIMIT),
        name="sb_attn_sample",
    )(q, pad(k_new), pad(v_new), cache_kt, cache_vt, tri, causal)


def _band_units(units):
    ahead = 2

    def attend(pv_fn, s):
        p = jnp.exp2(s - jnp.max(s, axis=-1, keepdims=True))
        den = jnp.sum(p, axis=-1, keepdims=True)
        return pv_fn(p.astype(BF16)) / den

    s_tiles = {c: units[c][0]() for c in range(min(ahead, len(units)))}
    outs = []
    for c, (_, pv_fn) in enumerate(units):
        outs.append(attend(pv_fn, s_tiles.pop(c)))
        if c + ahead < len(units):
            s_tiles[c + ahead] = units[c + ahead][0]()
    return outs


def _band_prompt_kernel(q_ref, k_ref, v_ref, bias_ref, o_ref, *, bq, n_sub, kw, back):
    nvar = bias_ref.shape[0]
    head_lanes = _head_masks((bq, LANES))

    def unit(u, h):
        blk = pl.program_id(2) * n_sub + u
        win = pl.ds(pl.multiple_of(jnp.maximum(blk * bq - back, 0), bq), kw)
        var = jnp.minimum(blk, nvar - 1)

        def scores():
            q = q_ref[u * bq:(u + 1) * bq, :]
            return _nt(jnp.where(head_lanes[h], q, jnp.zeros_like(q)), k_ref[win, :]) + bias_ref[var, h]

        return scores, lambda p: _nn(p, v_ref[win, :])

    outs = _band_units([unit(u, h) for u in range(n_sub) for h in range(2)])
    for u in range(n_sub):
        o_ref[u * bq:(u + 1) * bq, :] = jnp.where(head_lanes[0], outs[2 * u], outs[2 * u + 1])


def _band_sample_kernel(q_ref, kn_ref, vn_ref, ck_ref, cv_ref, bias_ref, o_ref):
    heads, _, rows = ck_ref.shape

    def unit(h):
        hl = slice(h * HEAD_DIM, (h + 1) * HEAD_DIM)

        def scores():
            qh = q_ref[:, hl]
            s = jnp.concatenate([_nn(qh, ck_ref[h].astype(BF16)), _nt(qh, kn_ref[:, hl])], axis=1)
            return s + bias_ref[h]

        def pv(p):
            return _nt(p[:, :rows], cv_ref[h].astype(BF16)) + _nn(p[:, rows:], vn_ref[:, hl])

        return scores, pv

    outs = _band_units([unit(h) for h in range(heads)])
    for h in range(heads):
        o_ref[:, h * HEAD_DIM:(h + 1) * HEAD_DIM] = outs[h]


def _band_attn_prompt(q, k, v, bias, *, bq, n_sub, kw, back):
    b, t, w = q.shape
    nvar = bias.shape[0]
    rows = bq * n_sub
    assert t % rows == 0 and w % LANES == 0
    return pl.pallas_call(
        functools.partial(_band_prompt_kernel, bq=bq, n_sub=n_sub, kw=kw, back=back),
        out_shape=jax.ShapeDtypeStruct((b, t, w), F32),
        grid=(b, w // LANES, t // rows),
        in_specs=[pl.BlockSpec((None, rows, LANES), lambda bi, hp, i: (bi, i, hp)),
                  pl.BlockSpec((None, t, LANES), lambda bi, hp, i: (bi, 0, hp)),
                  pl.BlockSpec((None, t, LANES), lambda bi, hp, i: (bi, 0, hp)),
                  pl.BlockSpec((nvar, 2, bq, kw), lambda bi, hp, i: (0, hp, 0, 0))],
        out_specs=pl.BlockSpec((None, rows, LANES), lambda bi, hp, i: (bi, i, hp)),
        compiler_params=pltpu.CompilerParams(
            dimension_semantics=("parallel", "parallel", "arbitrary"),
            vmem_limit_bytes=VMEM_LIMIT),
        name="band_attn",
    )(q, k, v, bias)


def _band_attn_sample(q, k_new, v_new, cache_kt, cache_vt, bias, layer):
    b, t, w = q.shape
    tn = k_new.shape[1]
    heads, rows = cache_kt.shape[2], cache_kt.shape[4]
    assert heads * HEAD_DIM == w and bias.shape == (heads, t, rows + tn)
    tok = lambda r: pl.BlockSpec((None, r, w), lambda bi: (bi, 0, 0))
    return pl.pallas_call(
        _band_sample_kernel,
        out_shape=jax.ShapeDtypeStruct((b, t, w), F32),
        grid=(b,),
        in_specs=[tok(t), tok(tn), tok(tn), _cache_spec(cache_kt, layer),
                  _cache_spec(cache_vt, layer), pl.BlockSpec(bias.shape, lambda bi: (0, 0, 0))],
        out_specs=tok(t),
        compiler_params=pltpu.CompilerParams(dimension_semantics=("arbitrary",),
                                             vmem_limit_bytes=VMEM_LIMIT),
        name="band_attn_sample",
    )(q, k_new, v_new, cache_kt, cache_vt, bias)


def _band_bias(rel_bias, offset, valid):
    rows, cols = valid.shape
    w = rows + cols
    m = jnp.arange(w)
    rel = jnp.clip(offset - jnp.where(m < cols, m, m - w), -REL_CLIP, REL_CLIP) + REL_CLIP
    base = rel_bias.astype(F32)[:, rel] * LOG2E
    tiled = jnp.tile(base, (1, rows))[:, :rows * (w - 1)].reshape(-1, rows, w - 1)
    return jnp.where(valid[None], tiled[:, :, :cols], NEG_INF)


def _prompt_band_bias(rel_bias, bq, kw, back):
    tiles = []
    for var in range(back // bq + 1):
        qc = (var * bq + jnp.arange(bq))[:, None] // CHUNK
        kc = jnp.arange(kw)[None, :] // CHUNK
        tiles.append(_band_bias(rel_bias, var * bq, (kc <= qc) & (kc >= qc - PAST_CHUNKS)))
    return jnp.stack(tiles)


def _out_ffn_kernel(h_ref, osb_ref, obd_ref, gsb_ref, gbd_ref, wout_ref, gffn_ref, wup_ref, wdn_ref,
                    gfin_ref, y_ref, *, w_sb, ff_chunk, apply_final):
    ysb = _rms(osb_ref[...], gsb_ref[...]).astype(BF16)
    ybd = _rms(obd_ref[...], gbd_ref[...]).astype(BF16)
    h1 = h_ref[...] + _nn(ysb, wout_ref[:w_sb, :]) + _nn(ybd, wout_ref[w_sb:, :])
    xn = _rms(h1, gffn_ref[...]).astype(BF16)
    mlp = None
    for c in range(wup_ref.shape[1] // ff_chunk):
        u = _nn(xn, wup_ref[:, c * ff_chunk:(c + 1) * ff_chunk])
        a = jnp.square(jnp.maximum(u, 0.0)).astype(BF16)
        d = _nn(a, wdn_ref[c * ff_chunk:(c + 1) * ff_chunk, :])
        mlp = d if mlp is None else mlp + d
    h2 = h1 + mlp
    y_ref[...] = _rms(h2, gfin_ref[...]) if apply_final else h2


def _out_ffn(h2, osb, obd, g_sb, g_bd, w_out, g_ffn, w_up, w_dn, g_fin, *, apply_final):
    n, d = h2.shape
    w_sb, w_bd = osb.shape[1], obd.shape[1]
    tm = min(ROW_TILE, n)
    assert n % tm == 0
    row = lambda i: (i, 0)
    const = lambda i: (0, 0)
    resident = lambda a: pl.BlockSpec(a.shape, const, pipeline_mode=pl.Buffered(1))
    vec = lambda a: a.reshape(1, -1).astype(F32)
    args = (h2, osb, obd, vec(g_sb), vec(g_bd), w_out, vec(g_ffn), w_up, w_dn, vec(g_fin))
    in_specs = [pl.BlockSpec((tm, d), row), pl.BlockSpec((tm, w_sb), row),
                pl.BlockSpec((tm, w_bd), row)] + [resident(a) for a in args[3:]]
    return pl.pallas_call(
        functools.partial(_out_ffn_kernel, w_sb=w_sb, ff_chunk=min(1024, w_up.shape[1]),
                          apply_final=apply_final),
        out_shape=jax.ShapeDtypeStruct((n, d), F32),
        grid=(n // tm,),
        in_specs=in_specs,
        out_specs=pl.BlockSpec((tm, d), row),
        compiler_params=pltpu.CompilerParams(dimension_semantics=("arbitrary",),
                                             vmem_limit_bytes=VMEM_LIMIT),
        name="out_ffn",
    )(*args)


def kernel(x_prompt, x_sample, cache_sb_k, cache_sb_v, cache_band_k, cache_band_v, norm_mix_g, w_in,
           rel_bias, norm_sb_g, norm_band_g, w_out, norm_ffn_g, w_up, w_down, norm_final_g):
    depth = w_in.shape[0]
    bsz, seq, d_model = x_prompt.shape
    dbsz, dseq, _ = x_sample.shape
    past = cache_sb_k.shape[2]
    band_rows = cache_band_k.shape[2]
    n_sb, n_bd = cache_sb_k.shape[3], cache_band_k.shape[3]
    w_sb, w_bd = n_sb * HEAD_DIM, n_bd * HEAD_DIM
    keep_p = min(PAST_CHUNKS * CHUNK, seq)
    back = PAST_CHUNKS * CHUNK
    assert w_sb == w_bd and w_in.shape[2] == 6 * w_sb
    assert seq % KEY_BLOCK == 0 and seq >= back + KEY_BLOCK and keep_p == ROW_TILE
    assert band_rows == back and past >= band_rows
    sb_qscale = LOG2E / math.sqrt(HEAD_DIM)
    bd_qscale = LOG2E / math.sqrt(HEAD_DIM)
    time_minor = lambda c: jnp.transpose(c, (0, 1, 3, 4, 2))

    h_p, h_s = x_prompt, x_sample
    outs = [[] for _ in range(8)]
    for l in range(depth):
        w_in_l = w_in[l].astype(BF16)
        w_out_l, w_up_l, w_dn_l = (w[l].astype(BF16) for w in (w_out, w_up, w_down))
        last = l == depth - 1
        tail = dict(g_sb=norm_sb_g[l], g_bd=norm_band_g[l], w_out=w_out_l, g_ffn=norm_ffn_g[l],
                    w_up=w_up_l, w_dn=w_dn_l, g_fin=norm_final_g, apply_final=last)

        n = bsz * seq
        (qsb, ksb, vsb, qbd, kbd, vbd, ksb32, vsb32, kbd32, vbd32) = _norm_proj(
            h_p.reshape(n, d_model), norm_mix_g[l], w_in_l, tail_period=seq // ROW_TILE,
            sb_qscale=sb_qscale, bd_qscale=bd_qscale)
        r3 = lambda a: a.reshape(bsz, -1, a.shape[-1])
        qsb, ksb, vsb, qbd, kbd, vbd = map(r3, (qsb, ksb, vsb, qbd, kbd, vbd))
        o_sb = _sb_attn_prompt(qsb, ksb, vsb, n_sub=SB_SUB_BLOCKS)
        kw = back + KEY_BLOCK
        o_bd = _band_attn_prompt(qbd, kbd, vbd, _prompt_band_bias(rel_bias[l], KEY_BLOCK, kw, back),
                                 bq=KEY_BLOCK, n_sub=BAND_SUB_BLOCKS, kw=kw, back=back)
        h_p = _out_ffn(h_p.reshape(n, d_model), o_sb.reshape(n, w_sb), o_bd.reshape(n, w_bd),
                       **tail).reshape(bsz, seq, d_model)
        outs[0].append(ksb32.reshape(bsz, seq, n_sb, HEAD_DIM))
        outs[1].append(vsb32.reshape(bsz, seq, n_sb, HEAD_DIM))
        outs[2].append(kbd32.reshape(bsz, keep_p, n_bd, HEAD_DIM))
        outs[3].append(vbd32.reshape(bsz, keep_p, n_bd, HEAD_DIM))

        n = dbsz * dseq
        (qsb, ksb, vsb, qbd, kbd, vbd, ksb32, vsb32, kbd32, vbd32) = _norm_proj(
            h_s.reshape(n, d_model), norm_mix_g[l], w_in_l, tail_period=None,
            sb_qscale=sb_qscale, bd_qscale=bd_qscale)
        r3 = lambda a: a.reshape(dbsz, -1, a.shape[-1])
        qsb, ksb, vsb, qbd, kbd, vbd = map(r3, (qsb, ksb, vsb, qbd, kbd, vbd))
        o_sb = _sb_attn_sample(qsb, ksb, vsb, time_minor(cache_sb_k), time_minor(cache_sb_v), l)
        new_rows = -(-dseq // LANES) * LANES
        k_pos = past - band_rows + jnp.arange(band_rows + new_rows)
        valid = (k_pos >= 0) & (k_pos < past + dseq)
        bias = _band_bias(rel_bias[l], band_rows, jnp.broadcast_to(valid[None, :], (dseq, valid.size)))
        pad = lambda a: jnp.pad(a, ((0, 0), (0, new_rows - dseq), (0, 0)))
        o_bd = _band_attn_sample(qbd, pad(kbd), pad(vbd), time_minor(cache_band_k),
                                 time_minor(cache_band_v), bias, l)
        h_s = _out_ffn(h_s.reshape(n, d_model), o_sb.reshape(n, w_sb), o_bd.reshape(n, w_bd),
                       **tail).reshape(dbsz, dseq, d_model)
        outs[4].append(ksb32.reshape(dbsz, dseq, n_sb, HEAD_DIM))
        outs[5].append(vsb32.reshape(dbsz, dseq, n_sb, HEAD_DIM))
        outs[6].append(kbd32.reshape(dbsz, dseq, n_bd, HEAD_DIM))
        outs[7].append(vbd32.reshape(dbsz, dseq, n_bd, HEAD_DIM))

    return (h_p, h_s) + tuple(jnp.stack(o) for o in outs)
```

```python
import functools
import math

import jax
import jax.numpy as jnp
from jax import lax
from jax.experimental import pallas as pl
from jax.experimental.pallas import tpu as pltpu

F32 = jnp.float32
BF16 = jnp.bfloat16

HEAD_DIM = 64
CHUNK = 64
PAST_CHUNKS = 8
REL_CLIP = 2 * CHUNK
EPS = 1e-6
NEG_INF = -1e30
LOG2E = 1.4426950408889634
SB_CUTOFF = 150.0
SB_EXP_CLAMP = 64.0

LANES = 128
KEY_BLOCK = 256
ROW_TILE = 512
BAND_SUB_BLOCKS = 4
SB_SUB_BLOCKS = 4
VMEM_LIMIT = 56 * 1024 * 1024

_NT = (((1,), (1,)), ((), ()))


def _rms(x, g):
    return x * lax.rsqrt(jnp.mean(x * x, axis=-1, keepdims=True) + EPS) * g


def _head_masks(shape):
    lane = lax.broadcasted_iota(jnp.int32, shape, 1)
    return lane < HEAD_DIM, lane >= HEAD_DIM


def _nt(a, b):
    return lax.dot_general(a, b, _NT, preferred_element_type=F32)


def _nn(a, b):
    return jnp.dot(a, b, preferred_element_type=F32)


def _norm_proj_kernel(x_ref, g_ref, w_ref, qsb_ref, ksb_ref, vsb_ref, qbd_ref, kbd_ref, vbd_ref,
                      ksb32_ref, vsb32_ref, kbd32_ref, vbd32_ref, *, width, tail_period, sb_qscale,
                      bd_qscale):
    xn = _rms(x_ref[...], g_ref[...]).astype(BF16)

    def proj(c):
        return _nn(xn, w_ref[:, c * width:(c + 1) * width])

    def split_heads(p):
        return p.reshape(p.shape[0], -1, HEAD_DIM)

    qsb_ref[...] = (proj(0) * sb_qscale).astype(BF16)
    p = proj(1)
    ksb_ref[...] = p.astype(BF16)
    ksb32_ref[...] = split_heads(p)
    p = proj(2)
    vsb_ref[...] = p.astype(BF16)
    vsb32_ref[...] = split_heads(p)
    qbd_ref[...] = (proj(3) * bd_qscale).astype(BF16)
    pk = proj(4)
    kbd_ref[...] = pk.astype(BF16)
    pv = proj(5)
    vbd_ref[...] = pv.astype(BF16)

    def write_band32():
        kbd32_ref[...] = split_heads(pk)
        vbd32_ref[...] = split_heads(pv)

    if tail_period is None:
        write_band32()
    else:
        pl.when(pl.program_id(0) % tail_period == tail_period - 1)(write_band32)


def _norm_proj(x2, g, w_bf16, *, tail_period, sb_qscale, bd_qscale):
    n, d = x2.shape
    width = w_bf16.shape[1] // 6
    tm = min(ROW_TILE, n)
    assert n % tm == 0
    nt = n // tm
    row = lambda i: (i, 0)
    const = lambda i: (0, 0)
    heads = width // HEAD_DIM
    if tail_period is None:
        band_rows, band_map = n, (lambda i: (i, 0, 0))
    else:
        assert nt % tail_period == 0
        band_rows, band_map = (nt // tail_period) * tm, (lambda i: (i // tail_period, 0, 0))
    blk = pl.BlockSpec((tm, width), row)
    blk32 = lambda m: pl.BlockSpec((tm, heads, HEAD_DIM), m)
    out_shape = ([jax.ShapeDtypeStruct((n, width), BF16)] * 6
                 + [jax.ShapeDtypeStruct((n, heads, HEAD_DIM), F32)] * 2
                 + [jax.ShapeDtypeStruct((band_rows, heads, HEAD_DIM), F32)] * 2)
    return pl.pallas_call(
        functools.partial(_norm_proj_kernel, width=width, tail_period=tail_period,
                          sb_qscale=sb_qscale, bd_qscale=bd_qscale),
        out_shape=out_shape,
        grid=(nt,),
        in_specs=[pl.BlockSpec((tm, d), row), pl.BlockSpec((1, d), const),
                  pl.BlockSpec(w_bf16.shape, const)],
        out_specs=[blk] * 6 + [blk32(lambda i: (i, 0, 0))] * 2 + [blk32(band_map)] * 2,
        compiler_params=pltpu.CompilerParams(dimension_semantics=("arbitrary",),
                                             vmem_limit_bytes=VMEM_LIMIT),
        name="norm_proj",
    )(x2, g.reshape(1, d), w_bf16)


def _sb_softplus2(z):
    return jnp.maximum(z, jnp.log2(1.0 + jnp.exp2(jnp.minimum(z, SB_EXP_CLAMP))))


def _sb_chains(chains, tri):
    n = len(chains)
    z, ls, excl, swept, pv = {}, {}, {}, {}, {}
    for i in range(n + 2):
        if i < n:
            score_fn, _, bias, _ = chains[i]
            z[i] = score_fn() if bias is None else score_fn() + bias
        if 0 <= i - 1 < n:
            c = i - 1
            zc = z.pop(c)
            sp = _sb_softplus2(zc)
            ls[c] = zc - sp
            excl[c] = _nn(sp.astype(BF16), tri)
            swept[c] = jnp.sum(sp, axis=-1, keepdims=True)
        if 0 <= i - 2 < n:
            c = i - 2
            _, pv_fn, _, carry_fn = chains[c]
            carry = carry_fn(swept)
            x = ls.pop(c) - excl.pop(c)
            pv[c] = pv_fn(jnp.exp2(x if carry is None else x - carry).astype(BF16))
    return [pv[c] for c in range(n)], [swept[c] for c in range(n)]


def _sb_sweep(streams, n_past, tri, causal, store):
    ids = range(len(streams))
    big = lambda ok: jnp.where(ok, 0.0, -NEG_INF).astype(F32)

    chains = []
    no_past = [big(n > 0) for n in n_past]
    for s in ids:
        diag, past = streams[s]
        chains.append((*diag, causal, lambda swept: None))
        chains.append((*past(jnp.maximum(n_past[s] - 1, 0)), None,
                       lambda swept, s=s: swept[2 * s] + no_past[s]))
    pv, swept = _sb_chains(chains, tri)
    store([pv[2 * s] + pv[2 * s + 1] for s in ids], False)
    carries = [swept[2 * s] + no_past[s] + swept[2 * s + 1] for s in ids]

    def least(cs):
        return jnp.min(functools.reduce(jnp.minimum, cs))

    most_past = functools.reduce(jnp.maximum, n_past)

    def cond(state):
        return jnp.logical_and(state[0] < most_past - 1, state[1] < SB_CUTOFF)

    def body(state):
        t, cs = state[0], state[2:]
        chains, start = [], []
        for s in ids:
            j = n_past[s] - 2 - t
            start.append(cs[s] + big(j >= 0))
            chains.append((*streams[s][1](jnp.maximum(j, 0)), None, lambda swept, c=start[-1]: c))
        pv, swept = _sb_chains(chains, tri)
        store(pv, True)
        new = [c + w for c, w in zip(start, swept)]
        return (t + 1, least(new), *new)

    lax.while_loop(cond, body, (0, least(carries), *carries))


def _sb_prompt_kernel(q_ref, kd_ref, vd_ref, kp_ref, vp_ref, tri_ref, causal_ref, o_ref, *, n_sub):
    rows = lambda u: slice(u * KEY_BLOCK, (u + 1) * KEY_BLOCK)
    head_lanes = _head_masks((KEY_BLOCK, LANES))
    streams, n_past = [], []
    for u in range(n_sub):
        q = q_ref[rows(u), :]
        for m in head_lanes:
            qh = jnp.where(m, q, jnp.zeros_like(q))

            def past(j, qh=qh):
                blk = pl.ds(pl.multiple_of(j * KEY_BLOCK, KEY_BLOCK), KEY_BLOCK)
                return (lambda: _nt(qh, kp_ref[blk, :])), (lambda w: _nn(w, vp_ref[blk, :]))

            diag = ((lambda qh=qh, u=u: _nt(qh, kd_ref[rows(u), :])),
                    (lambda w, u=u: _nn(w, vd_ref[rows(u), :])))
            streams.append((diag, past))
            n_past.append(pl.program_id(2) * n_sub + u)

    def store(pv, accumulate):
        for u in range(n_sub):
            val = jnp.where(head_lanes[0], pv[2 * u], pv[2 * u + 1])
            o_ref[rows(u), :] = o_ref[rows(u), :] + val if accumulate else val

    _sb_sweep(streams, n_past, tri_ref[...], causal_ref[...], store)


def _sb_sample_kernel(q_ref, kn_ref, vn_ref, ck_ref, cv_ref, tri_ref, causal_ref, o_ref, *, n_past):
    heads = ck_ref.shape[0]
    streams = []
    for h in range(heads):
        hl = slice(h * HEAD_DIM, (h + 1) * HEAD_DIM)
        qh = q_ref[:, hl]

        def past(j, h=h, qh=qh):
            blk = pl.ds(pl.multiple_of(j * KEY_BLOCK, KEY_BLOCK), KEY_BLOCK)
            return ((lambda: _nn(qh, ck_ref[h, :, blk].astype(BF16))),
                    (lambda w: _nt(w, cv_ref[h, :, blk].astype(BF16))))

        diag = ((lambda qh=qh, hl=hl: _nt(qh, kn_ref[:, hl])), (lambda w, hl=hl: _nn(w, vn_ref[:, hl])))
        streams.append((diag, past))

    def store(pv, accumulate):
        for h in range(heads):
            hl = slice(h * HEAD_DIM, (h + 1) * HEAD_DIM)
            o_ref[:, hl] = o_ref[:, hl] + pv[h] if accumulate else pv[h]

    _sb_sweep(streams, [n_past] * heads, tri_ref[...], causal_ref[...], store)


def _sb_constants(bq):
    row = lax.broadcasted_iota(jnp.int32, (KEY_BLOCK, KEY_BLOCK), 0)
    col = lax.broadcasted_iota(jnp.int32, (KEY_BLOCK, KEY_BLOCK), 1)
    tri = (row > col).astype(BF16)
    causal = jnp.where(col < row, 0.0, NEG_INF).astype(F32)[:bq]
    return tri, causal


def _sb_attn_prompt(q, k, v, *, n_sub):
    b, t, w = q.shape
    rows = KEY_BLOCK * n_sub
    assert t % rows == 0 and w % LANES == 0
    tri, causal = _sb_constants(KEY_BLOCK)
    qmap = lambda bi, hp, i: (bi, i, hp)
    pmap = lambda bi, hp, i: (bi, 0, hp)
    const = lambda bi, hp, i: (0, 0)
    return pl.pallas_call(
        functools.partial(_sb_prompt_kernel, n_sub=n_sub),
        out_shape=jax.ShapeDtypeStruct((b, t, w), F32),
        grid=(b, w // LANES, t // rows),
        in_specs=[pl.BlockSpec((None, rows, LANES), qmap),
                  pl.BlockSpec((None, rows, LANES), qmap),
                  pl.BlockSpec((None, rows, LANES), qmap),
                  pl.BlockSpec((None, t, LANES), pmap),
                  pl.BlockSpec((None, t, LANES), pmap),
                  pl.BlockSpec((KEY_BLOCK, KEY_BLOCK), const),
                  pl.BlockSpec((KEY_BLOCK, KEY_BLOCK), const)],
        out_specs=pl.BlockSpec((None, rows, LANES), qmap),
        compiler_params=pltpu.CompilerParams(
            dimension_semantics=("parallel", "parallel", "arbitrary"),
            vmem_limit_bytes=VMEM_LIMIT),
        name="sb_attn",
    )(q, k, v, k, v, tri, causal)


def _cache_spec(cache_t, layer):
    return pl.BlockSpec((None, None) + cache_t.shape[2:], lambda bi: (layer, bi, 0, 0, 0))


def _sb_attn_sample(q, k_new, v_new, cache_kt, cache_vt, layer):
    b, t, w = q.shape
    heads, past = cache_kt.shape[2], cache_kt.shape[4]
    assert t <= KEY_BLOCK and past % KEY_BLOCK == 0 and heads * HEAD_DIM == w
    tri, causal = _sb_constants(t)
    pad = lambda a: jnp.pad(a, ((0, 0), (0, KEY_BLOCK - t), (0, 0)))
    tok = lambda rows: pl.BlockSpec((None, rows, w), lambda bi: (bi, 0, 0))
    const = lambda bi: (0, 0)
    return pl.pallas_call(
        functools.partial(_sb_sample_kernel, n_past=past // KEY_BLOCK),
        out_shape=jax.ShapeDtypeStruct((b, t, w), F32),
        grid=(b,),
        in_specs=[tok(t), tok(KEY_BLOCK), tok(KEY_BLOCK), _cache_spec(cache_kt, layer),
                  _cache_spec(cache_vt, layer), pl.BlockSpec((KEY_BLOCK, KEY_BLOCK), const),
                  pl.BlockSpec((t, KEY_BLOCK), const)],
        out_specs=tok(t),
        compiler_params=pltpu.CompilerParams(dimension_semantics=("arbitrary",),
                                             vmem_limit_bytes=VMEM_LIMIT),
        name="sb_attn_sample",
    )(q, pad(k_new), pad(v_new), cache_kt, cache_vt, tri, causal)


def _band_units(units):
    ahead = 2

    def attend(pv_fn, s):
        p = jnp.exp2(s - jnp.max(s, axis=-1, keepdims=True))
        den = jnp.sum(p, axis=-1, keepdims=True)
        return pv_fn(p.astype(BF16)) / den

    s_tiles = {c: units[c][0]() for c in range(min(ahead, len(units)))}
    outs = []
    for c, (_, pv_fn) in enumerate(units):
        outs.append(attend(pv_fn, s_tiles.pop(c)))
        if c + ahead < len(units):
            s_tiles[c + ahead] = units[c + ahead][0]()
    return outs


def _band_prompt_kernel(q_ref, k_ref, v_ref, bias_ref, o_ref, *, bq, n_sub, kw, back):
    head_lanes = _head_masks((bq, LANES))

    def unit(u, h):
        first = (pl.program_id(2) * n_sub + u) * bq - back
        start = jnp.maximum(first, 0)
        win = pl.ds(pl.multiple_of(start, bq), kw)
        cols = pl.ds(pl.multiple_of(start - first, bq), kw)

        def scores():
            q = q_ref[u * bq:(u + 1) * bq, :]
            qh = jnp.where(head_lanes[h], q, jnp.zeros_like(q))
            return _nt(qh, k_ref[win, :]) + bias_ref[h, :, cols]

        return scores, lambda p: _nn(p, v_ref[win, :])

    outs = _band_units([unit(u, h) for u in range(n_sub) for h in range(2)])
    for u in range(n_sub):
        o_ref[u * bq:(u + 1) * bq, :] = jnp.where(head_lanes[0], outs[2 * u], outs[2 * u + 1])


def _band_sample_kernel(q_ref, kn_ref, vn_ref, ck_ref, cv_ref, bias_ref, o_ref):
    heads, _, rows = ck_ref.shape

    def unit(h):
        hl = slice(h * HEAD_DIM, (h + 1) * HEAD_DIM)

        def scores():
            qh = q_ref[:, hl]
            s = jnp.concatenate([_nn(qh, ck_ref[h].astype(BF16)), _nt(qh, kn_ref[:, hl])], axis=1)
            return s + bias_ref[h]

        def pv(p):
            return _nt(p[:, :rows], cv_ref[h].astype(BF16)) + _nn(p[:, rows:], vn_ref[:, hl])

        return scores, pv

    outs = _band_units([unit(h) for h in range(heads)])
    for h in range(heads):
        o_ref[:, h * HEAD_DIM:(h + 1) * HEAD_DIM] = outs[h]


def _band_attn_prompt(q, k, v, bias, *, bq, n_sub, kw, back):
    b, t, w = q.shape
    rows = bq * n_sub
    assert t % rows == 0 and w % LANES == 0 and back % bq == 0 and bias.shape[1:] == (bq, back + kw)
    return pl.pallas_call(
        functools.partial(_band_prompt_kernel, bq=bq, n_sub=n_sub, kw=kw, back=back),
        out_shape=jax.ShapeDtypeStruct((b, t, w), F32),
        grid=(b, w // LANES, t // rows),
        in_specs=[pl.BlockSpec((None, rows, LANES), lambda bi, hp, i: (bi, i, hp)),
                  pl.BlockSpec((None, t, LANES), lambda bi, hp, i: (bi, 0, hp)),
                  pl.BlockSpec((None, t, LANES), lambda bi, hp, i: (bi, 0, hp)),
                  pl.BlockSpec((2, bq, back + kw), lambda bi, hp, i: (hp, 0, 0))],
        out_specs=pl.BlockSpec((None, rows, LANES), lambda bi, hp, i: (bi, i, hp)),
        compiler_params=pltpu.CompilerParams(
            dimension_semantics=("parallel", "parallel", "arbitrary"),
            vmem_limit_bytes=VMEM_LIMIT),
        name="band_attn",
    )(q, k, v, bias)


def _band_attn_sample(q, k_new, v_new, cache_kt, cache_vt, bias, layer):
    b, t, w = q.shape
    tn = k_new.shape[1]
    heads, rows = cache_kt.shape[2], cache_kt.shape[4]
    assert heads * HEAD_DIM == w and bias.shape == (heads, t, rows + tn)
    tok = lambda r: pl.BlockSpec((None, r, w), lambda bi: (bi, 0, 0))
    return pl.pallas_call(
        _band_sample_kernel,
        out_shape=jax.ShapeDtypeStruct((b, t, w), F32),
        grid=(b,),
        in_specs=[tok(t), tok(tn), tok(tn), _cache_spec(cache_kt, layer),
                  _cache_spec(cache_vt, layer), pl.BlockSpec(bias.shape, lambda bi: (0, 0, 0))],
        out_specs=tok(t),
        compiler_params=pltpu.CompilerParams(dimension_semantics=("arbitrary",),
                                             vmem_limit_bytes=VMEM_LIMIT),
        name="band_attn_sample",
    )(q, k_new, v_new, cache_kt, cache_vt, bias)


def _band_bias(rel_bias, offset, valid):
    rows, cols = valid.shape
    w = rows + cols
    m = jnp.arange(w)
    rel = jnp.clip(offset - jnp.where(m < cols, m, m - w), -REL_CLIP, REL_CLIP) + REL_CLIP
    base = rel_bias.astype(F32)[:, rel] * LOG2E
    tiled = jnp.tile(base, (1, rows))[:, :rows * (w - 1)].reshape(-1, rows, w - 1)
    return jnp.where(valid[None], tiled[:, :, :cols], NEG_INF)


def _prompt_band_bias(rel_bias, bq, kw, back):
    qc = (back + jnp.arange(bq))[:, None] // CHUNK
    kc = jnp.arange(back + kw)[None, :] // CHUNK
    return _band_bias(rel_bias, back, (kc <= qc) & (kc >= qc - PAST_CHUNKS))


def _out_ffn_kernel(h_ref, osb_ref, obd_ref, gsb_ref, gbd_ref, wout_ref, gffn_ref, wup_ref, wdn_ref,
                    gfin_ref, y_ref, *, w_sb, ff_chunk, apply_final):
    ysb = _rms(osb_ref[...], gsb_ref[...]).astype(BF16)
    ybd = _rms(obd_ref[...], gbd_ref[...]).astype(BF16)
    h1 = h_ref[...] + _nn(ysb, wout_ref[:w_sb, :]) + _nn(ybd, wout_ref[w_sb:, :])
    xn = _rms(h1, gffn_ref[...]).astype(BF16)
    mlp = None
    for c in range(wup_ref.shape[1] // ff_chunk):
        u = _nn(xn, wup_ref[:, c * ff_chunk:(c + 1) * ff_chunk])
        a = jnp.square(jnp.maximum(u, 0.0)).astype(BF16)
        d = _nn(a, wdn_ref[c * ff_chunk:(c + 1) * ff_chunk, :])
        mlp = d if mlp is None else mlp + d
    h2 = h1 + mlp
    y_ref[...] = _rms(h2, gfin_ref[...]) if apply_final else h2


def _out_ffn(h2, osb, obd, g_sb, g_bd, w_out, g_ffn, w_up, w_dn, g_fin, *, apply_final):
    n, d = h2.shape
    w_sb, w_bd = osb.shape[1], obd.shape[1]
    tm = min(ROW_TILE, n)
    assert n % tm == 0
    row = lambda i: (i, 0)
    const = lambda i: (0, 0)
    resident = lambda a: pl.BlockSpec(a.shape, const, pipeline_mode=pl.Buffered(1))
    vec = lambda a: a.reshape(1, -1).astype(F32)
    args = (h2, osb, obd, vec(g_sb), vec(g_bd), w_out, vec(g_ffn), w_up, w_dn, vec(g_fin))
    in_specs = [pl.BlockSpec((tm, d), row), pl.BlockSpec((tm, w_sb), row),
                pl.BlockSpec((tm, w_bd), row)] + [resident(a) for a in args[3:]]
    return pl.pallas_call(
        functools.partial(_out_ffn_kernel, w_sb=w_sb, ff_chunk=min(1024, w_up.shape[1]),
                          apply_final=apply_final),
        out_shape=jax.ShapeDtypeStruct((n, d), F32),
        grid=(n // tm,),
        in_specs=in_specs,
        out_specs=pl.BlockSpec((tm, d), row),
        compiler_params=pltpu.CompilerParams(dimension_semantics=("arbitrary",),
                                             vmem_limit_bytes=VMEM_LIMIT),
        name="out_ffn",
    )(*args)


def kernel(x_prompt, x_sample, cache_sb_k, cache_sb_v, cache_band_k, cache_band_v, norm_mix_g, w_in,
           rel_bias, norm_sb_g, norm_band_g, w_out, norm_ffn_g, w_up, w_down, norm_final_g):
    depth = w_in.shape[0]
    bsz, seq, d_model = x_prompt.shape
    dbsz, dseq, _ = x_sample.shape
    past = cache_sb_k.shape[2]
    band_rows = cache_band_k.shape[2]
    n_sb, n_bd = cache_sb_k.shape[3], cache_band_k.shape[3]
    w_sb, w_bd = n_sb * HEAD_DIM, n_bd * HEAD_DIM
    keep_p = min(PAST_CHUNKS * CHUNK, seq)
    back = PAST_CHUNKS * CHUNK
    assert w_sb == w_bd and w_in.shape[2] == 6 * w_sb
    assert seq % KEY_BLOCK == 0 and seq >= back + KEY_BLOCK and keep_p == ROW_TILE
    assert band_rows == back and past >= band_rows
    sb_qscale = LOG2E / math.sqrt(HEAD_DIM)
    bd_qscale = LOG2E / math.sqrt(HEAD_DIM)
    time_minor = lambda c: jnp.transpose(c, (0, 1, 3, 4, 2))

    h_p, h_s = x_prompt, x_sample
    outs = [[] for _ in range(8)]
    for l in range(depth):
        w_in_l = w_in[l].astype(BF16)
        w_out_l, w_up_l, w_dn_l = (w[l].astype(BF16) for w in (w_out, w_up, w_down))
        last = l == depth - 1
        tail = dict(g_sb=norm_sb_g[l], g_bd=norm_band_g[l], w_out=w_out_l, g_ffn=norm_ffn_g[l],
                    w_up=w_up_l, w_dn=w_dn_l, g_fin=norm_final_g, apply_final=last)

        n = bsz * seq
        (qsb, ksb, vsb, qbd, kbd, vbd, ksb32, vsb32, kbd32, vbd32) = _norm_proj(
            h_p.reshape(n, d_model), norm_mix_g[l], w_in_l, tail_period=seq // ROW_TILE,
            sb_qscale=sb_qscale, bd_qscale=bd_qscale)
        r3 = lambda a: a.reshape(bsz, -1, a.shape[-1])
        qsb, ksb, vsb, qbd, kbd, vbd = map(r3, (qsb, ksb, vsb, qbd, kbd, vbd))
        o_sb = _sb_attn_prompt(qsb, ksb, vsb, n_sub=SB_SUB_BLOCKS)
        kw = back + KEY_BLOCK
        o_bd = _band_attn_prompt(qbd, kbd, vbd, _prompt_band_bias(rel_bias[l], KEY_BLOCK, kw, back),
                                 bq=KEY_BLOCK, n_sub=BAND_SUB_BLOCKS, kw=kw, back=back)
        h_p = _out_ffn(h_p.reshape(n, d_model), o_sb.reshape(n, w_sb), o_bd.reshape(n, w_bd),
                       **tail).reshape(bsz, seq, d_model)
        outs[0].append(ksb32.reshape(bsz, seq, n_sb, HEAD_DIM))
        outs[1].append(vsb32.reshape(bsz, seq, n_sb, HEAD_DIM))
        outs[2].append(kbd32.reshape(bsz, keep_p, n_bd, HEAD_DIM))
        outs[3].append(vbd32.reshape(bsz, keep_p, n_bd, HEAD_DIM))

        n = dbsz * dseq
        (qsb, ksb, vsb, qbd, kbd, vbd, ksb32, vsb32, kbd32, vbd32) = _norm_proj(
            h_s.reshape(n, d_model), norm_mix_g[l], w_in_l, tail_period=None,
            sb_qscale=sb_qscale, bd_qscale=bd_qscale)
        r3 = lambda a: a.reshape(dbsz, -1, a.shape[-1])
        qsb, ksb, vsb, qbd, kbd, vbd = map(r3, (qsb, ksb, vsb, qbd, kbd, vbd))
        o_sb = _sb_attn_sample(qsb, ksb, vsb, time_minor(cache_sb_k), time_minor(cache_sb_v), l)
        new_rows = -(-dseq // LANES) * LANES
        k_pos = past - band_rows + jnp.arange(band_rows + new_rows)
        valid = (k_pos >= 0) & (k_pos < past + dseq)
        bias = _band_bias(rel_bias[l], band_rows, jnp.broadcast_to(valid[None, :], (dseq, valid.size)))
        pad = lambda a: jnp.pad(a, ((0, 0), (0, new_rows - dseq), (0, 0)))
        o_bd = _band_attn_sample(qbd, pad(kbd), pad(vbd), time_minor(cache_band_k),
                                 time_minor(cache_band_v), bias, l)
        h_s = _out_ffn(h_s.reshape(n, d_model), o_sb.reshape(n, w_sb), o_bd.reshape(n, w_bd),
                       **tail).reshape(dbsz, dseq, d_model)
        outs[4].append(ksb32.reshape(dbsz, dseq, n_sb, HEAD_DIM))
        outs[5].append(vsb32.reshape(dbsz, dseq, n_sb, HEAD_DIM))
        outs[6].append(kbd32.reshape(dbsz, dseq, n_bd, HEAD_DIM))
        outs[7].append(vbd32.reshape(dbsz, dseq, n_bd, HEAD_DIM))

    return (h_p, h_s) + tuple(jnp.stack(o) for o in outs)
```

```python
import functools
import math

import jax
import jax.numpy as jnp
from jax import lax
from jax.experimental import pallas as pl
from jax.experimental.pallas import tpu as pltpu

F32 = jnp.float32
BF16 = jnp.bfloat16

HEAD_DIM = 64
CHUNK = 64
PAST_CHUNKS = 8
REL_CLIP = 2 * CHUNK
EPS = 1e-6
NEG_INF = -1e30
LOG2E = 1.4426950408889634
SB_CUTOFF = 150.0
SB_EXP_CLAMP = 64.0

LANES = 128
KEY_BLOCK = 256
ROW_TILE = 512
BAND_SUB_BLOCKS = 8
SB_SUB_BLOCKS = 4
VMEM_LIMIT = 56 * 1024 * 1024

_NT = (((1,), (1,)), ((), ()))


def _rms(x, g):
    return x * lax.rsqrt(jnp.mean(x * x, axis=-1, keepdims=True) + EPS) * g


def _head_masks(shape):
    lane = lax.broadcasted_iota(jnp.int32, shape, 1)
    return lane < HEAD_DIM, lane >= HEAD_DIM


def _nt(a, b):
    return lax.dot_general(a, b, _NT, preferred_element_type=F32)


def _nn(a, b):
    return jnp.dot(a, b, preferred_element_type=F32)


def _norm_proj_kernel(x_ref, g_ref, w_ref, qsb_ref, ksb_ref, vsb_ref, qbd_ref, kbd_ref, vbd_ref,
                      ksb32_ref, vsb32_ref, kbd32_ref, vbd32_ref, *, width, tail_period, sb_qscale,
                      bd_qscale):
    xn = _rms(x_ref[...], g_ref[...]).astype(BF16)

    def proj(c):
        return _nn(xn, w_ref[:, c * width:(c + 1) * width])

    def split_heads(p):
        return p.reshape(p.shape[0], -1, HEAD_DIM)

    qsb_ref[...] = (proj(0) * sb_qscale).astype(BF16)
    p = proj(1)
    ksb_ref[...] = p.astype(BF16)
    ksb32_ref[...] = split_heads(p)
    p = proj(2)
    vsb_ref[...] = p.astype(BF16)
    vsb32_ref[...] = split_heads(p)
    qbd_ref[...] = (proj(3) * bd_qscale).astype(BF16)
    pk = proj(4)
    kbd_ref[...] = pk.astype(BF16)
    pv = proj(5)
    vbd_ref[...] = pv.astype(BF16)

    def write_band32():
        kbd32_ref[...] = split_heads(pk)
        vbd32_ref[...] = split_heads(pv)

    if tail_period is None:
        write_band32()
    else:
        pl.when(pl.program_id(0) % tail_period == tail_period - 1)(write_band32)


def _norm_proj(x2, g, w_bf16, *, tail_period, sb_qscale, bd_qscale):
    n, d = x2.shape
    width = w_bf16.shape[1] // 6
    tm = min(ROW_TILE, n)
    assert n % tm == 0
    nt = n // tm
    row = lambda i: (i, 0)
    const = lambda i: (0, 0)
    heads = width // HEAD_DIM
    if tail_period is None:
        band_rows, band_map = n, (lambda i: (i, 0, 0))
    else:
        assert nt % tail_period == 0
        band_rows, band_map = (nt // tail_period) * tm, (lambda i: (i // tail_period, 0, 0))
    blk = pl.BlockSpec((tm, width), row)
    blk32 = lambda m: pl.BlockSpec((tm, heads, HEAD_DIM), m)
    out_shape = ([jax.ShapeDtypeStruct((n, width), BF16)] * 6
                 + [jax.ShapeDtypeStruct((n, heads, HEAD_DIM), F32)] * 2
                 + [jax.ShapeDtypeStruct((band_rows, heads, HEAD_DIM), F32)] * 2)
    return pl.pallas_call(
        functools.partial(_norm_proj_kernel, width=width, tail_period=tail_period,
                          sb_qscale=sb_qscale, bd_qscale=bd_qscale),
        out_shape=out_shape,
        grid=(nt,),
        in_specs=[pl.BlockSpec((tm, d), row), pl.BlockSpec((1, d), const),
                  pl.BlockSpec(w_bf16.shape, const)],
        out_specs=[blk] * 6 + [blk32(lambda i: (i, 0, 0))] * 2 + [blk32(band_map)] * 2,
        compiler_params=pltpu.CompilerParams(dimension_semantics=("arbitrary",),
                                             vmem_limit_bytes=VMEM_LIMIT),
        name="norm_proj",
    )(x2, g.reshape(1, d), w_bf16)


def _sb_softplus2(z):
    return jnp.maximum(z, jnp.log2(1.0 + jnp.exp2(jnp.minimum(z, SB_EXP_CLAMP))))


def _run_skewed(tasks):
    depth = max(len(t) for t in tasks)
    vals = [None] * len(tasks)
    for i in range(len(tasks) + depth - 1):
        for k in range(depth):
            j = i - k
            if 0 <= j < len(tasks) and k < len(tasks[j]):
                vals[j] = tasks[j][k](vals[j])
    return vals


def _below_diagonal(fn, *tiles):
    h, w = tiles[0].shape[0] // 2, tiles[0].shape[1] // 2
    top = fn(*[t[:h, :w] for t in tiles])
    bottom = fn(*[t[h:, :] for t in tiles])
    return jnp.concatenate([jnp.concatenate([top, jnp.zeros_like(top)], axis=1), bottom], axis=0)


def _sb_tasks(chains, tri, swept):
    def task(score_fn, pv_fn, bias, carry_fn):
        square = bias is not None and bias.shape[0] == bias.shape[1]
        on = _below_diagonal if square else (lambda fn, *tiles: fn(*tiles))

        def scores(_):
            return score_fn() if bias is None else on(jnp.add, score_fn(), bias)

        def suffix(z):
            sp = on(_sb_softplus2, z)
            swept.append(jnp.sum(sp, axis=-1, keepdims=True))
            return on(jnp.subtract, z, sp), _nn(sp.astype(BF16), tri)

        def weigh(stage):
            carry = carry_fn(swept)
            if carry is None:
                w = on(lambda ls, excl: jnp.exp2(ls - excl).astype(BF16), *stage)
            else:
                w = jnp.exp2(stage[0] - stage[1] - carry).astype(BF16)
            return pv_fn(w)

        return [scores, suffix, weigh]

    return [task(*c) for c in chains]


def _sb_sweep(streams, n_past, tri, causal, store):
    ids = range(len(streams))
    big = lambda ok: jnp.where(ok, 0.0, -NEG_INF).astype(F32)

    chains = []
    no_past = [big(n > 0) for n in n_past]
    for s in ids:
        diag, past = streams[s]
        chains.append((*diag, causal, lambda swept: None))
        chains.append((*past(jnp.maximum(n_past[s] - 1, 0)), None,
                       lambda swept, s=s: swept[2 * s] + no_past[s]))
    swept = []
    pv = _run_skewed(_sb_tasks(chains, tri, swept))
    store([pv[2 * s] + pv[2 * s + 1] for s in ids], False)
    carries = [swept[2 * s] + no_past[s] + swept[2 * s + 1] for s in ids]

    def least(cs):
        return jnp.min(functools.reduce(jnp.minimum, cs))

    most_past = functools.reduce(jnp.maximum, n_past)

    def cond(state):
        return jnp.logical_and(state[0] < most_past - 1, state[1] < SB_CUTOFF)

    def body(state):
        t, cs = state[0], state[2:]
        chains, start = [], []
        for s in ids:
            j = n_past[s] - 2 - t
            start.append(cs[s] + big(j >= 0))
            chains.append((*streams[s][1](jnp.maximum(j, 0)), None, lambda swept, c=start[-1]: c))
        swept = []
        store(_run_skewed(_sb_tasks(chains, tri, swept)), True)
        new = [c + w for c, w in zip(start, swept)]
        return (t + 1, least(new), *new)

    lax.while_loop(cond, body, (0, least(carries), *carries))


def _sb_sample_kernel(q_ref, kn_ref, vn_ref, ck_ref, cv_ref, tri_ref, causal_ref, o_ref, *, n_past):
    heads = ck_ref.shape[0]
    streams = []
    for h in range(heads):
        hl = slice(h * HEAD_DIM, (h + 1) * HEAD_DIM)
        qh = q_ref[:, hl]

        def past(j, h=h, qh=qh):
            blk = pl.ds(pl.multiple_of(j * KEY_BLOCK, KEY_BLOCK), KEY_BLOCK)
            return ((lambda: _nn(qh, ck_ref[h, :, blk].astype(BF16))),
                    (lambda w: _nt(w, cv_ref[h, :, blk].astype(BF16))))

        diag = ((lambda qh=qh, hl=hl: _nt(qh, kn_ref[:, hl])), (lambda w, hl=hl: _nn(w, vn_ref[:, hl])))
        streams.append((diag, past))

    def store(pv, accumulate):
        for h in range(heads):
            hl = slice(h * HEAD_DIM, (h + 1) * HEAD_DIM)
            o_ref[:, hl] = o_ref[:, hl] + pv[h] if accumulate else pv[h]

    _sb_sweep(streams, [n_past] * heads, tri_ref[...], causal_ref[...], store)


def _sb_constants(bq):
    row = lax.broadcasted_iota(jnp.int32, (KEY_BLOCK, KEY_BLOCK), 0)
    col = lax.broadcasted_iota(jnp.int32, (KEY_BLOCK, KEY_BLOCK), 1)
    tri = (row > col).astype(BF16)
    causal = jnp.where(col < row, 0.0, NEG_INF).astype(F32)[:bq]
    return tri, causal


def _cache_spec(cache_t, layer):
    return pl.BlockSpec((None, None) + cache_t.shape[2:], lambda bi: (layer, bi, 0, 0, 0))


def _sb_attn_sample(q, k_new, v_new, cache_kt, cache_vt, layer):
    b, t, w = q.shape
    heads, past = cache_kt.shape[2], cache_kt.shape[4]
    assert t <= KEY_BLOCK and past % KEY_BLOCK == 0 and heads * HEAD_DIM == w
    tri, causal = _sb_constants(t)
    pad = lambda a: jnp.pad(a, ((0, 0), (0, KEY_BLOCK - t), (0, 0)))
    tok = lambda rows: pl.BlockSpec((None, rows, w), lambda bi: (bi, 0, 0))
    const = lambda bi: (0, 0)
    return pl.pallas_call(
        functools.partial(_sb_sample_kernel, n_past=past // KEY_BLOCK),
        out_shape=jax.ShapeDtypeStruct((b, t, w), F32),
        grid=(b,),
        in_specs=[tok(t), tok(KEY_BLOCK), tok(KEY_BLOCK), _cache_spec(cache_kt, layer),
                  _cache_spec(cache_vt, layer), pl.BlockSpec((KEY_BLOCK, KEY_BLOCK), const),
                  pl.BlockSpec((t, KEY_BLOCK), const)],
        out_specs=tok(t),
        compiler_params=pltpu.CompilerParams(dimension_semantics=("arbitrary",),
                                             vmem_limit_bytes=VMEM_LIMIT),
        name="sb_attn_sample",
    )(q, pad(k_new), pad(v_new), cache_kt, cache_vt, tri, causal)


def _band_tasks(units):
    def task(score_fn, pv_fn):
        def attend(s):
            p = jnp.exp2(s - jnp.max(s, axis=-1, keepdims=True))
            den = jnp.sum(p, axis=-1, keepdims=True)
            return pv_fn(p.astype(BF16)) / den

        return [lambda _: score_fn(), lambda s: s, attend]

    return [task(*u) for u in units]


def _band_sample_kernel(q_ref, kn_ref, vn_ref, ck_ref, cv_ref, bias_ref, o_ref):
    heads, _, rows = ck_ref.shape

    def unit(h):
        hl = slice(h * HEAD_DIM, (h + 1) * HEAD_DIM)

        def scores():
            qh = q_ref[:, hl]
            s = jnp.concatenate([_nn(qh, ck_ref[h].astype(BF16)), _nt(qh, kn_ref[:, hl])], axis=1)
            return s + bias_ref[h]

        def pv(p):
            return _nt(p[:, :rows], cv_ref[h].astype(BF16)) + _nn(p[:, rows:], vn_ref[:, hl])

        return scores, pv

    outs = _run_skewed(_band_tasks([unit(h) for h in range(heads)]))
    for h in range(heads):
        o_ref[:, h * HEAD_DIM:(h + 1) * HEAD_DIM] = outs[h]


def _band_attn_sample(q, k_new, v_new, cache_kt, cache_vt, bias, layer):
    b, t, w = q.shape
    tn = k_new.shape[1]
    heads, rows = cache_kt.shape[2], cache_kt.shape[4]
    assert heads * HEAD_DIM == w and bias.shape == (heads, t, rows + tn)
    tok = lambda r: pl.BlockSpec((None, r, w), lambda bi: (bi, 0, 0))
    return pl.pallas_call(
        _band_sample_kernel,
        out_shape=jax.ShapeDtypeStruct((b, t, w), F32),
        grid=(b,),
        in_specs=[tok(t), tok(tn), tok(tn), _cache_spec(cache_kt, layer),
                  _cache_spec(cache_vt, layer), pl.BlockSpec(bias.shape, lambda bi: (0, 0, 0))],
        out_specs=tok(t),
        compiler_params=pltpu.CompilerParams(dimension_semantics=("arbitrary",),
                                             vmem_limit_bytes=VMEM_LIMIT),
        name="band_attn_sample",
    )(q, k_new, v_new, cache_kt, cache_vt, bias)


def _sb_prompt_kernel(q_ref, kd_ref, vd_ref, kp_ref, vp_ref, tri_ref, causal_ref, o_ref, *, n_sub):
    rows = lambda u: slice(u * KEY_BLOCK, (u + 1) * KEY_BLOCK)
    head_lanes = _head_masks((KEY_BLOCK, LANES))
    streams, n_past = [], []
    for u in range(n_sub):
        q = q_ref[rows(u), :]
        for m in head_lanes:
            qh = jnp.where(m, q, jnp.zeros_like(q))

            def past(j, qh=qh):
                blk = pl.ds(pl.multiple_of(j * KEY_BLOCK, KEY_BLOCK), KEY_BLOCK)
                return (lambda: _nt(qh, kp_ref[blk, :])), (lambda w: _nn(w, vp_ref[blk, :]))

            diag = ((lambda qh=qh, u=u: _nt(qh, kd_ref[rows(u), :])),
                    (lambda w, u=u: _nn(w, vd_ref[rows(u), :])))
            streams.append((diag, past))
            n_past.append(pl.program_id(2) * n_sub + u)

    def store(pv, accumulate):
        for u in range(n_sub):
            val = jnp.where(head_lanes[0], pv[2 * u], pv[2 * u + 1])
            o_ref[rows(u), :] = o_ref[rows(u), :] + val if accumulate else val

    _sb_sweep(streams, n_past, tri_ref[...], causal_ref[...], store)


def _sb_attn_prompt(q, k, v, *, n_sub):
    b, t, w = q.shape
    rows = KEY_BLOCK * n_sub
    assert t % rows == 0 and w % LANES == 0
    tri, causal = _sb_constants(KEY_BLOCK)
    qmap = lambda bi, hp, i: (bi, i, hp)
    pmap = lambda bi, hp, i: (bi, 0, hp)
    const = lambda bi, hp, i: (0, 0)
    return pl.pallas_call(
        functools.partial(_sb_prompt_kernel, n_sub=n_sub),
        out_shape=jax.ShapeDtypeStruct((b, t, w), F32),
        grid=(b, w // LANES, t // rows),
        in_specs=[pl.BlockSpec((None, rows, LANES), qmap),
                  pl.BlockSpec((None, rows, LANES), qmap),
                  pl.BlockSpec((None, rows, LANES), qmap),
                  pl.BlockSpec((None, t, LANES), pmap),
                  pl.BlockSpec((None, t, LANES), pmap),
                  pl.BlockSpec((KEY_BLOCK, KEY_BLOCK), const),
                  pl.BlockSpec((KEY_BLOCK, KEY_BLOCK), const)],
        out_specs=pl.BlockSpec((None, rows, LANES), qmap),
        compiler_params=pltpu.CompilerParams(
            dimension_semantics=("parallel", "parallel", "arbitrary"),
            vmem_limit_bytes=VMEM_LIMIT),
        name="sb_attn",
    )(q, k, v, k, v, tri, causal)


def _band_prompt_kernel(q_ref, k_ref, v_ref, bias_ref, o_ref, *, bq, n_sub, kw, back):
    head_lanes = _head_masks((bq, LANES))

    def unit(u, h):
        first = (pl.program_id(2) * n_sub + u) * bq - back
        start = jnp.maximum(first, 0)
        win = pl.ds(pl.multiple_of(start, bq), kw)
        cols = pl.ds(pl.multiple_of(start - first, bq), kw)

        def scores():
            q = q_ref[u * bq:(u + 1) * bq, :]
            qh = jnp.where(head_lanes[h], q, jnp.zeros_like(q))
            return _nt(qh, k_ref[win, :]) + bias_ref[h, :, cols]

        return scores, lambda p: _nn(p, v_ref[win, :])

    outs = _run_skewed(_band_tasks([unit(u, h) for u in range(n_sub) for h in range(2)]))
    for u in range(n_sub):
        o_ref[u * bq:(u + 1) * bq, :] = jnp.where(head_lanes[0], outs[2 * u], outs[2 * u + 1])


def _band_attn_prompt(q, k, v, bias, *, bq, n_sub, kw, back):
    b, t, w = q.shape
    rows = bq * n_sub
    assert t % rows == 0 and w % LANES == 0 and back % bq == 0 and bias.shape[1:] == (bq, back + kw)
    return pl.pallas_call(
        functools.partial(_band_prompt_kernel, bq=bq, n_sub=n_sub, kw=kw, back=back),
        out_shape=jax.ShapeDtypeStruct((b, t, w), F32),
        grid=(b, w // LANES, t // rows),
        in_specs=[pl.BlockSpec((None, rows, LANES), lambda bi, hp, i: (bi, i, hp)),
                  pl.BlockSpec((None, t, LANES), lambda bi, hp, i: (bi, 0, hp)),
                  pl.BlockSpec((None, t, LANES), lambda bi, hp, i: (bi, 0, hp)),
                  pl.BlockSpec((2, bq, back + kw), lambda bi, hp, i: (hp, 0, 0))],
        out_specs=pl.BlockSpec((None, rows, LANES), lambda bi, hp, i: (bi, i, hp)),
        compiler_params=pltpu.CompilerParams(
            dimension_semantics=("parallel", "parallel", "arbitrary"),
            vmem_limit_bytes=VMEM_LIMIT),
        name="band_attn",
    )(q, k, v, bias)


def _band_bias(rel_bias, offset, valid):
    rows, cols = valid.shape
    w = rows + cols
    m = jnp.arange(w)
    rel = jnp.clip(offset - jnp.where(m < cols, m, m - w), -REL_CLIP, REL_CLIP) + REL_CLIP
    base = rel_bias.astype(F32)[:, rel] * LOG2E
    tiled = jnp.tile(base, (1, rows))[:, :rows * (w - 1)].reshape(-1, rows, w - 1)
    return jnp.where(valid[None], tiled[:, :, :cols], NEG_INF)


def _prompt_band_bias(rel_bias, bq, kw, back):
    qc = (back + jnp.arange(bq))[:, None] // CHUNK
    kc = jnp.arange(back + kw)[None, :] // CHUNK
    return _band_bias(rel_bias, back, (kc <= qc) & (kc >= qc - PAST_CHUNKS))


def _out_ffn_kernel(h_ref, osb_ref, obd_ref, gsb_ref, gbd_ref, wout_ref, gffn_ref, wup_ref, wdn_ref,
                    gfin_ref, y_ref, *, w_sb, ff_chunk, apply_final):
    ysb = _rms(osb_ref[...], gsb_ref[...]).astype(BF16)
    ybd = _rms(obd_ref[...], gbd_ref[...]).astype(BF16)
    h1 = h_ref[...] + _nn(ysb, wout_ref[:w_sb, :]) + _nn(ybd, wout_ref[w_sb:, :])
    xn = _rms(h1, gffn_ref[...]).astype(BF16)
    mlp = None
    for c in range(wup_ref.shape[1] // ff_chunk):
        u = _nn(xn, wup_ref[:, c * ff_chunk:(c + 1) * ff_chunk])
        a = jnp.square(jnp.maximum(u, 0.0)).astype(BF16)
        d = _nn(a, wdn_ref[c * ff_chunk:(c + 1) * ff_chunk, :])
        mlp = d if mlp is None else mlp + d
    h2 = h1 + mlp
    y_ref[...] = _rms(h2, gfin_ref[...]) if apply_final else h2


def _out_ffn(h2, osb, obd, g_sb, g_bd, w_out, g_ffn, w_up, w_dn, g_fin, *, apply_final):
    n, d = h2.shape
    w_sb, w_bd = osb.shape[1], obd.shape[1]
    tm = min(ROW_TILE, n)
    assert n % tm == 0
    row = lambda i: (i, 0)
    const = lambda i: (0, 0)
    resident = lambda a: pl.BlockSpec(a.shape, const, pipeline_mode=pl.Buffered(1))
    vec = lambda a: a.reshape(1, -1).astype(F32)
    args = (h2, osb, obd, vec(g_sb), vec(g_bd), w_out, vec(g_ffn), w_up, w_dn, vec(g_fin))
    in_specs = [pl.BlockSpec((tm, d), row), pl.BlockSpec((tm, w_sb), row),
                pl.BlockSpec((tm, w_bd), row)] + [resident(a) for a in args[3:]]
    return pl.pallas_call(
        functools.partial(_out_ffn_kernel, w_sb=w_sb, ff_chunk=min(1024, w_up.shape[1]),
                          apply_final=apply_final),
        out_shape=jax.ShapeDtypeStruct((n, d), F32),
        grid=(n // tm,),
        in_specs=in_specs,
        out_specs=pl.BlockSpec((tm, d), row),
        compiler_params=pltpu.CompilerParams(dimension_semantics=("arbitrary",),
                                             vmem_limit_bytes=VMEM_LIMIT),
        name="out_ffn",
    )(*args)


def kernel(x_prompt, x_sample, cache_sb_k, cache_sb_v, cache_band_k, cache_band_v, norm_mix_g, w_in,
           rel_bias, norm_sb_g, norm_band_g, w_out, norm_ffn_g, w_up, w_down, norm_final_g):
    depth = w_in.shape[0]
    bsz, seq, d_model = x_prompt.shape
    dbsz, dseq, _ = x_sample.shape
    past = cache_sb_k.shape[2]
    band_rows = cache_band_k.shape[2]
    n_sb, n_bd = cache_sb_k.shape[3], cache_band_k.shape[3]
    w_sb, w_bd = n_sb * HEAD_DIM, n_bd * HEAD_DIM
    keep_p = min(PAST_CHUNKS * CHUNK, seq)
    back = PAST_CHUNKS * CHUNK
    assert w_sb == w_bd and w_in.shape[2] == 6 * w_sb
    assert seq % KEY_BLOCK == 0 and seq >= back + KEY_BLOCK and keep_p == ROW_TILE
    assert band_rows == back and past >= band_rows
    sb_qscale = LOG2E / math.sqrt(HEAD_DIM)
    bd_qscale = LOG2E / math.sqrt(HEAD_DIM)
    time_minor = lambda c: jnp.transpose(c, (0, 1, 3, 4, 2))

    h_p, h_s = x_prompt, x_sample
    outs = [[] for _ in range(8)]
    for l in range(depth):
        w_in_l = w_in[l].astype(BF16)
        w_out_l, w_up_l, w_dn_l = (w[l].astype(BF16) for w in (w_out, w_up, w_down))
        last = l == depth - 1
        tail = dict(g_sb=norm_sb_g[l], g_bd=norm_band_g[l], w_out=w_out_l, g_ffn=norm_ffn_g[l],
                    w_up=w_up_l, w_dn=w_dn_l, g_fin=norm_final_g, apply_final=last)

        n = bsz * seq
        (qsb, ksb, vsb, qbd, kbd, vbd, ksb32, vsb32, kbd32, vbd32) = _norm_proj(
            h_p.reshape(n, d_model), norm_mix_g[l], w_in_l, tail_period=seq // ROW_TILE,
            sb_qscale=sb_qscale, bd_qscale=bd_qscale)
        r3 = lambda a: a.reshape(bsz, -1, a.shape[-1])
        qsb, ksb, vsb, qbd, kbd, vbd = map(r3, (qsb, ksb, vsb, qbd, kbd, vbd))
        o_sb = _sb_attn_prompt(qsb, ksb, vsb, n_sub=SB_SUB_BLOCKS)
        kw = back + KEY_BLOCK
        o_bd = _band_attn_prompt(qbd, kbd, vbd, _prompt_band_bias(rel_bias[l], KEY_BLOCK, kw, back),
                                 bq=KEY_BLOCK, n_sub=BAND_SUB_BLOCKS, kw=kw, back=back)
        h_p = _out_ffn(h_p.reshape(n, d_model), o_sb.reshape(n, w_sb), o_bd.reshape(n, w_bd),
                       **tail).reshape(bsz, seq, d_model)
        outs[0].append(ksb32.reshape(bsz, seq, n_sb, HEAD_DIM))
        outs[1].append(vsb32.reshape(bsz, seq, n_sb, HEAD_DIM))
        outs[2].append(kbd32.reshape(bsz, keep_p, n_bd, HEAD_DIM))
        outs[3].append(vbd32.reshape(bsz, keep_p, n_bd, HEAD_DIM))

        n = dbsz * dseq
        (qsb, ksb, vsb, qbd, kbd, vbd, ksb32, vsb32, kbd32, vbd32) = _norm_proj(
            h_s.reshape(n, d_model), norm_mix_g[l], w_in_l, tail_period=None,
            sb_qscale=sb_qscale, bd_qscale=bd_qscale)
        r3 = lambda a: a.reshape(dbsz, -1, a.shape[-1])
        qsb, ksb, vsb, qbd, kbd, vbd = map(r3, (qsb, ksb, vsb, qbd, kbd, vbd))
        o_sb = _sb_attn_sample(qsb, ksb, vsb, time_minor(cache_sb_k), time_minor(cache_sb_v), l)
        new_rows = -(-dseq // LANES) * LANES
        k_pos = past - band_rows + jnp.arange(band_rows + new_rows)
        valid = (k_pos >= 0) & (k_pos < past + dseq)
        bias = _band_bias(rel_bias[l], band_rows, jnp.broadcast_to(valid[None, :], (dseq, valid.size)))
        pad = lambda a: jnp.pad(a, ((0, 0), (0, new_rows - dseq), (0, 0)))
        o_bd = _band_attn_sample(qbd, pad(kbd), pad(vbd), time_minor(cache_band_k),
                                 time_minor(cache_band_v), bias, l)
        h_s = _out_ffn(h_s.reshape(n, d_model), o_sb.reshape(n, w_sb), o_bd.reshape(n, w_bd),
                       **tail).reshape(dbsz, dseq, d_model)
        outs[4].append(ksb32.reshape(dbsz, dseq, n_sb, HEAD_DIM))
        outs[5].append(vsb32.reshape(dbsz, dseq, n_sb, HEAD_DIM))
        outs[6].append(kbd32.reshape(dbsz, dseq, n_bd, HEAD_DIM))
        outs[7].append(vbd32.reshape(dbsz, dseq, n_bd, HEAD_DIM))

    return (h_p, h_s) + tuple(jnp.stack(o) for o in outs)
```

```python
import functools
import math

import jax
import jax.numpy as jnp
from jax import lax
from jax.experimental import pallas as pl
from jax.experimental.pallas import tpu as pltpu

F32 = jnp.float32
BF16 = jnp.bfloat16

HEAD_DIM = 64
CHUNK = 64
PAST_CHUNKS = 8
REL_CLIP = 2 * CHUNK
EPS = 1e-6
NEG_INF = -1e30
LOG2E = 1.4426950408889634
SB_CUTOFF = 150.0
SB_EXP_CLAMP = 64.0

LANES = 128
KEY_BLOCK = 256
ROW_TILE = 512
BAND_SUB_BLOCKS = 16
SB_SUB_BLOCKS = 8
VMEM_LIMIT = 56 * 1024 * 1024

_NT = (((1,), (1,)), ((), ()))


def _rms(x, g):
    return x * lax.rsqrt(jnp.mean(x * x, axis=-1, keepdims=True) + EPS) * g


def _head_masks(shape):
    lane = lax.broadcasted_iota(jnp.int32, shape, 1)
    return lane < HEAD_DIM, lane >= HEAD_DIM


def _nt(a, b):
    return lax.dot_general(a, b, _NT, preferred_element_type=F32)


def _nn(a, b):
    return jnp.dot(a, b, preferred_element_type=F32)


def _norm_proj_kernel(x_ref, g_ref, w_ref, qsb_ref, ksb_ref, vsb_ref, qbd_ref, kbd_ref, vbd_ref,
                      ksb32_ref, vsb32_ref, kbd32_ref, vbd32_ref, *, width, tail_period, sb_qscale,
                      bd_qscale):
    xn = _rms(x_ref[...], g_ref[...]).astype(BF16)

    def proj(c):
        return _nn(xn, w_ref[:, c * width:(c + 1) * width])

    def split_heads(p):
        return p.reshape(p.shape[0], -1, HEAD_DIM)

    qsb_ref[...] = (proj(0) * sb_qscale).astype(BF16)
    p = proj(1)
    ksb_ref[...] = p.astype(BF16)
    ksb32_ref[...] = split_heads(p)
    p = proj(2)
    vsb_ref[...] = p.astype(BF16)
    vsb32_ref[...] = split_heads(p)
    qbd_ref[...] = (proj(3) * bd_qscale).astype(BF16)
    pk = proj(4)
    kbd_ref[...] = pk.astype(BF16)
    pv = proj(5)
    vbd_ref[...] = pv.astype(BF16)

    def write_band32():
        kbd32_ref[...] = split_heads(pk)
        vbd32_ref[...] = split_heads(pv)

    if tail_period is None:
        write_band32()
    else:
        pl.when(pl.program_id(0) % tail_period == tail_period - 1)(write_band32)


def _norm_proj(x2, g, w_bf16, *, tail_period, sb_qscale, bd_qscale):
    n, d = x2.shape
    width = w_bf16.shape[1] // 6
    tm = min(ROW_TILE, n)
    assert n % tm == 0
    nt = n // tm
    row = lambda i: (i, 0)
    const = lambda i: (0, 0)
    heads = width // HEAD_DIM
    if tail_period is None:
        band_rows, band_map = n, (lambda i: (i, 0, 0))
    else:
        assert nt % tail_period == 0
        band_rows, band_map = (nt // tail_period) * tm, (lambda i: (i // tail_period, 0, 0))
    blk = pl.BlockSpec((tm, width), row)
    blk32 = lambda m: pl.BlockSpec((tm, heads, HEAD_DIM), m)
    out_shape = ([jax.ShapeDtypeStruct((n, width), BF16)] * 6
                 + [jax.ShapeDtypeStruct((n, heads, HEAD_DIM), F32)] * 2
                 + [jax.ShapeDtypeStruct((band_rows, heads, HEAD_DIM), F32)] * 2)
    return pl.pallas_call(
        functools.partial(_norm_proj_kernel, width=width, tail_period=tail_period,
                          sb_qscale=sb_qscale, bd_qscale=bd_qscale),
        out_shape=out_shape,
        grid=(nt,),
        in_specs=[pl.BlockSpec((tm, d), row), pl.BlockSpec((1, d), const),
                  pl.BlockSpec(w_bf16.shape, const)],
        out_specs=[blk] * 6 + [blk32(lambda i: (i, 0, 0))] * 2 + [blk32(band_map)] * 2,
        compiler_params=pltpu.CompilerParams(dimension_semantics=("arbitrary",),
                                             vmem_limit_bytes=VMEM_LIMIT),
        name="norm_proj",
    )(x2, g.reshape(1, d), w_bf16)


def _sb_softplus2(z):
    return jnp.maximum(z, jnp.log2(1.0 + jnp.exp2(jnp.minimum(z, SB_EXP_CLAMP))))


def _run_skewed(tasks):
    depth = max(len(t) for t in tasks)
    vals = [None] * len(tasks)
    for i in range(len(tasks) + depth - 1):
        for k in range(depth):
            j = i - k
            if 0 <= j < len(tasks) and k < len(tasks[j]):
                vals[j] = tasks[j][k](vals[j])
    return vals


def _below_diagonal(fn, *tiles):
    h, w = tiles[0].shape[0] // 2, tiles[0].shape[1] // 2
    top = fn(*[t[:h, :w] for t in tiles])
    bottom = fn(*[t[h:, :] for t in tiles])
    return jnp.concatenate([jnp.concatenate([top, jnp.zeros_like(top)], axis=1), bottom], axis=0)


def _sb_tasks(chains, tri, swept):
    def task(score_fn, pv_fn, bias, carry_fn):
        square = bias is not None and bias.shape[0] == bias.shape[1]
        on = _below_diagonal if square else (lambda fn, *tiles: fn(*tiles))

        def scores(_):
            return score_fn() if bias is None else on(jnp.add, score_fn(), bias)

        def suffix(z):
            sp = on(_sb_softplus2, z)
            swept.append(jnp.sum(sp, axis=-1, keepdims=True))
            return on(jnp.subtract, z, sp), _nn(sp.astype(BF16), tri)

        def weigh(stage):
            carry = carry_fn(swept)
            if carry is None:
                w = on(lambda ls, excl: jnp.exp2(ls - excl).astype(BF16), *stage)
            else:
                w = jnp.exp2(stage[0] - stage[1] - carry).astype(BF16)
            return pv_fn(w)

        return [scores, suffix, weigh]

    return [task(*c) for c in chains]


def _sb_sweep(streams, n_past, tri, causal, store):
    ids = range(len(streams))
    big = lambda ok: jnp.where(ok, 0.0, -NEG_INF).astype(F32)

    chains = []
    no_past = [big(n > 0) for n in n_past]
    for s in ids:
        diag, past = streams[s]
        chains.append((*diag, causal, lambda swept: None))
        chains.append((*past(jnp.maximum(n_past[s] - 1, 0)), None,
                       lambda swept, s=s: swept[2 * s] + no_past[s]))
    swept = []
    pv = _run_skewed(_sb_tasks(chains, tri, swept))
    store([pv[2 * s] + pv[2 * s + 1] for s in ids], False)
    carries = [swept[2 * s] + no_past[s] + swept[2 * s + 1] for s in ids]

    def least(cs):
        return jnp.min(functools.reduce(jnp.minimum, cs))

    most_past = functools.reduce(jnp.maximum, n_past)

    def cond(state):
        return jnp.logical_and(state[0] < most_past - 1, state[1] < SB_CUTOFF)

    def body(state):
        t, cs = state[0], state[2:]
        chains, start = [], []
        for s in ids:
            j = n_past[s] - 2 - t
            start.append(cs[s] + big(j >= 0))
            chains.append((*streams[s][1](jnp.maximum(j, 0)), None, lambda swept, c=start[-1]: c))
        swept = []
        store(_run_skewed(_sb_tasks(chains, tri, swept)), True)
        new = [c + w for c, w in zip(start, swept)]
        return (t + 1, least(new), *new)

    lax.while_loop(cond, body, (0, least(carries), *carries))


def _sb_sample_kernel(q_ref, kn_ref, vn_ref, ck_ref, cv_ref, tri_ref, causal_ref, o_ref, *, n_past):
    heads = ck_ref.shape[0]
    streams = []
    for h in range(heads):
        hl = slice(h * HEAD_DIM, (h + 1) * HEAD_DIM)
        qh = q_ref[:, hl]

        def past(j, h=h, qh=qh):
            blk = pl.ds(pl.multiple_of(j * KEY_BLOCK, KEY_BLOCK), KEY_BLOCK)
            return ((lambda: _nn(qh, ck_ref[h, :, blk].astype(BF16))),
                    (lambda w: _nt(w, cv_ref[h, :, blk].astype(BF16))))

        diag = ((lambda qh=qh, hl=hl: _nt(qh, kn_ref[:, hl])), (lambda w, hl=hl: _nn(w, vn_ref[:, hl])))
        streams.append((diag, past))

    def store(pv, accumulate):
        for h in range(heads):
            hl = slice(h * HEAD_DIM, (h + 1) * HEAD_DIM)
            o_ref[:, hl] = o_ref[:, hl] + pv[h] if accumulate else pv[h]

    _sb_sweep(streams, [n_past] * heads, tri_ref[...], causal_ref[...], store)


def _sb_constants(bq):
    row = lax.broadcasted_iota(jnp.int32, (KEY_BLOCK, KEY_BLOCK), 0)
    col = lax.broadcasted_iota(jnp.int32, (KEY_BLOCK, KEY_BLOCK), 1)
    tri = (row > col).astype(BF16)
    causal = jnp.where(col < row, 0.0, NEG_INF).astype(F32)[:bq]
    return tri, causal


def _cache_spec(cache_t, layer):
    return pl.BlockSpec((None, None) + cache_t.shape[2:], lambda bi: (layer, bi, 0, 0, 0))


def _sb_attn_sample(q, k_new, v_new, cache_kt, cache_vt, layer):
    b, t, w = q.shape
    heads, past = cache_kt.shape[2], cache_kt.shape[4]
    assert t <= KEY_BLOCK and past % KEY_BLOCK == 0 and heads * HEAD_DIM == w
    tri, causal = _sb_constants(t)
    pad = lambda a: jnp.pad(a, ((0, 0), (0, KEY_BLOCK - t), (0, 0)))
    tok = lambda rows: pl.BlockSpec((None, rows, w), lambda bi: (bi, 0, 0))
    const = lambda bi: (0, 0)
    return pl.pallas_call(
        functools.partial(_sb_sample_kernel, n_past=past // KEY_BLOCK),
        out_shape=jax.ShapeDtypeStruct((b, t, w), F32),
        grid=(b,),
        in_specs=[tok(t), tok(KEY_BLOCK), tok(KEY_BLOCK), _cache_spec(cache_kt, layer),
                  _cache_spec(cache_vt, layer), pl.BlockSpec((KEY_BLOCK, KEY_BLOCK), const),
                  pl.BlockSpec((t, KEY_BLOCK), const)],
        out_specs=tok(t),
        compiler_params=pltpu.CompilerParams(dimension_semantics=("arbitrary",),
                                             vmem_limit_bytes=VMEM_LIMIT),
        name="sb_attn_sample",
    )(q, pad(k_new), pad(v_new), cache_kt, cache_vt, tri, causal)


def _band_tasks(units):
    def task(score_fn, pv_fn):
        def attend(s):
            p = jnp.exp2(s - jnp.max(s, axis=-1, keepdims=True))
            den = jnp.sum(p, axis=-1, keepdims=True)
            return pv_fn(p.astype(BF16)) / den

        return [lambda _: score_fn(), lambda s: s, attend]

    return [task(*u) for u in units]


def _band_sample_kernel(q_ref, kn_ref, vn_ref, ck_ref, cv_ref, bias_ref, o_ref):
    heads, _, rows = ck_ref.shape

    def unit(h):
        hl = slice(h * HEAD_DIM, (h + 1) * HEAD_DIM)

        def scores():
            qh = q_ref[:, hl]
            s = jnp.concatenate([_nn(qh, ck_ref[h].astype(BF16)), _nt(qh, kn_ref[:, hl])], axis=1)
            return s + bias_ref[h]

        def pv(p):
            return _nt(p[:, :rows], cv_ref[h].astype(BF16)) + _nn(p[:, rows:], vn_ref[:, hl])

        return scores, pv

    outs = _run_skewed(_band_tasks([unit(h) for h in range(heads)]))
    for h in range(heads):
        o_ref[:, h * HEAD_DIM:(h + 1) * HEAD_DIM] = outs[h]


def _band_attn_sample(q, k_new, v_new, cache_kt, cache_vt, bias, layer):
    b, t, w = q.shape
    tn = k_new.shape[1]
    heads, rows = cache_kt.shape[2], cache_kt.shape[4]
    assert heads * HEAD_DIM == w and bias.shape == (heads, t, rows + tn)
    tok = lambda r: pl.BlockSpec((None, r, w), lambda bi: (bi, 0, 0))
    return pl.pallas_call(
        _band_sample_kernel,
        out_shape=jax.ShapeDtypeStruct((b, t, w), F32),
        grid=(b,),
        in_specs=[tok(t), tok(tn), tok(tn), _cache_spec(cache_kt, layer),
                  _cache_spec(cache_vt, layer), pl.BlockSpec(bias.shape, lambda bi: (0, 0, 0))],
        out_specs=tok(t),
        compiler_params=pltpu.CompilerParams(dimension_semantics=("arbitrary",),
                                             vmem_limit_bytes=VMEM_LIMIT),
        name="band_attn_sample",
    )(q, k_new, v_new, cache_kt, cache_vt, bias)


def _sb_prompt_kernel(q_ref, kd_ref, vd_ref, kp_ref, vp_ref, tri_ref, causal_ref, o_ref, *, n_sub):
    rows = lambda u: slice(u * KEY_BLOCK, (u + 1) * KEY_BLOCK)
    head_lanes = _head_masks((KEY_BLOCK, LANES))
    streams, n_past = [], []
    for u in range(n_sub):
        q = q_ref[rows(u), :]
        for m in head_lanes:
            qh = jnp.where(m, q, jnp.zeros_like(q))

            def past(j, qh=qh):
                blk = pl.ds(pl.multiple_of(j * KEY_BLOCK, KEY_BLOCK), KEY_BLOCK)
                return (lambda: _nt(qh, kp_ref[blk, :])), (lambda w: _nn(w, vp_ref[blk, :]))

            diag = ((lambda qh=qh, u=u: _nt(qh, kd_ref[rows(u), :])),
                    (lambda w, u=u: _nn(w, vd_ref[rows(u), :])))
            streams.append((diag, past))
            n_past.append(pl.program_id(2) * n_sub + u)

    def store(pv, accumulate):
        for u in range(n_sub):
            val = jnp.where(head_lanes[0], pv[2 * u], pv[2 * u + 1])
            o_ref[rows(u), :] = o_ref[rows(u), :] + val if accumulate else val

    _sb_sweep(streams, n_past, tri_ref[...], causal_ref[...], store)


def _sb_attn_prompt(q, k, v, *, n_sub):
    b, t, w = q.shape
    rows = KEY_BLOCK * n_sub
    assert t % rows == 0 and w % LANES == 0
    tri, causal = _sb_constants(KEY_BLOCK)
    qmap = lambda bi, hp, i: (bi, i, hp)
    pmap = lambda bi, hp, i: (bi, 0, hp)
    const = lambda bi, hp, i: (0, 0)
    return pl.pallas_call(
        functools.partial(_sb_prompt_kernel, n_sub=n_sub),
        out_shape=jax.ShapeDtypeStruct((b, t, w), F32),
        grid=(b, w // LANES, t // rows),
        in_specs=[pl.BlockSpec((None, rows, LANES), qmap),
                  pl.BlockSpec((None, rows, LANES), qmap),
                  pl.BlockSpec((None, rows, LANES), qmap),
                  pl.BlockSpec((None, t, LANES), pmap),
                  pl.BlockSpec((None, t, LANES), pmap),
                  pl.BlockSpec((KEY_BLOCK, KEY_BLOCK), const),
                  pl.BlockSpec((KEY_BLOCK, KEY_BLOCK), const)],
        out_specs=pl.BlockSpec((None, rows, LANES), qmap),
        compiler_params=pltpu.CompilerParams(
            dimension_semantics=("parallel", "parallel", "arbitrary"),
            vmem_limit_bytes=VMEM_LIMIT),
        name="sb_attn",
    )(q, k, v, k, v, tri, causal)


def _band_prompt_kernel(q_ref, k_ref, v_ref, bias_ref, o_ref, *, bq, n_sub, kw, back):
    head_lanes = _head_masks((bq, LANES))

    def unit(u, h):
        first = (pl.program_id(2) * n_sub + u) * bq - back
        start = jnp.maximum(first, 0)
        win = pl.ds(pl.multiple_of(start, bq), kw)
        cols = pl.ds(pl.multiple_of(start - first, bq), kw)

        def scores():
            q = q_ref[u * bq:(u + 1) * bq, :]
            qh = jnp.where(head_lanes[h], q, jnp.zeros_like(q))
            return _nt(qh, k_ref[win, :]) + bias_ref[h, :, cols]

        return scores, lambda p: _nn(p, v_ref[win, :])

    outs = _run_skewed(_band_tasks([unit(u, h) for u in range(n_sub) for h in range(2)]))
    for u in range(n_sub):
        o_ref[u * bq:(u + 1) * bq, :] = jnp.where(head_lanes[0], outs[2 * u], outs[2 * u + 1])


def _band_attn_prompt(q, k, v, bias, *, bq, n_sub, kw, back):
    b, t, w = q.shape
    rows = bq * n_sub
    assert t % rows == 0 and w % LANES == 0 and back % bq == 0 and bias.shape[1:] == (bq, back + kw)
    return pl.pallas_call(
        functools.partial(_band_prompt_kernel, bq=bq, n_sub=n_sub, kw=kw, back=back),
        out_shape=jax.ShapeDtypeStruct((b, t, w), F32),
        grid=(b, w // LANES, t // rows),
        in_specs=[pl.BlockSpec((None, rows, LANES), lambda bi, hp, i: (bi, i, hp)),
                  pl.BlockSpec((None, t, LANES), lambda bi, hp, i: (bi, 0, hp)),
                  pl.BlockSpec((None, t, LANES), lambda bi, hp, i: (bi, 0, hp)),
                  pl.BlockSpec((2, bq, back + kw), lambda bi, hp, i: (hp, 0, 0))],
        out_specs=pl.BlockSpec((None, rows, LANES), lambda bi, hp, i: (bi, i, hp)),
        compiler_params=pltpu.CompilerParams(
            dimension_semantics=("parallel", "parallel", "arbitrary"),
            vmem_limit_bytes=VMEM_LIMIT),
        name="band_attn",
    )(q, k, v, bias)


def _band_bias(rel_bias, offset, valid):
    rows, cols = valid.shape
    w = rows + cols
    m = jnp.arange(w)
    rel = jnp.clip(offset - jnp.where(m < cols, m, m - w), -REL_CLIP, REL_CLIP) + REL_CLIP
    base = rel_bias.astype(F32)[:, rel] * LOG2E
    tiled = jnp.tile(base, (1, rows))[:, :rows * (w - 1)].reshape(-1, rows, w - 1)
    return jnp.where(valid[None], tiled[:, :, :cols], NEG_INF)


def _prompt_band_bias(rel_bias, bq, kw, back):
    qc = (back + jnp.arange(bq))[:, None] // CHUNK
    kc = jnp.arange(back + kw)[None, :] // CHUNK
    return _band_bias(rel_bias, back, (kc <= qc) & (kc >= qc - PAST_CHUNKS))


def _out_ffn_kernel(h_ref, osb_ref, obd_ref, gsb_ref, gbd_ref, wout_ref, gffn_ref, wup_ref, wdn_ref,
                    gfin_ref, y_ref, *, w_sb, ff_chunk, apply_final):
    ysb = _rms(osb_ref[...], gsb_ref[...]).astype(BF16)
    ybd = _rms(obd_ref[...], gbd_ref[...]).astype(BF16)
    h1 = h_ref[...] + _nn(ysb, wout_ref[:w_sb, :]) + _nn(ybd, wout_ref[w_sb:, :])
    xn = _rms(h1, gffn_ref[...]).astype(BF16)
    mlp = None
    for c in range(wup_ref.shape[1] // ff_chunk):
        u = _nn(xn, wup_ref[:, c * ff_chunk:(c + 1) * ff_chunk])
        a = jnp.square(jnp.maximum(u, 0.0)).astype(BF16)
        d = _nn(a, wdn_ref[c * ff_chunk:(c + 1) * ff_chunk, :])
        mlp = d if mlp is None else mlp + d
    h2 = h1 + mlp
    y_ref[...] = _rms(h2, gfin_ref[...]) if apply_final else h2


def _out_ffn(h2, osb, obd, g_sb, g_bd, w_out, g_ffn, w_up, w_dn, g_fin, *, apply_final):
    n, d = h2.shape
    w_sb, w_bd = osb.shape[1], obd.shape[1]
    tm = min(ROW_TILE, n)
    assert n % tm == 0
    row = lambda i: (i, 0)
    const = lambda i: (0, 0)
    resident = lambda a: pl.BlockSpec(a.shape, const, pipeline_mode=pl.Buffered(1))
    vec = lambda a: a.reshape(1, -1).astype(F32)
    args = (h2, osb, obd, vec(g_sb), vec(g_bd), w_out, vec(g_ffn), w_up, w_dn, vec(g_fin))
    in_specs = [pl.BlockSpec((tm, d), row), pl.BlockSpec((tm, w_sb), row),
                pl.BlockSpec((tm, w_bd), row)] + [resident(a) for a in args[3:]]
    return pl.pallas_call(
        functools.partial(_out_ffn_kernel, w_sb=w_sb, ff_chunk=min(1024, w_up.shape[1]),
                          apply_final=apply_final),
        out_shape=jax.ShapeDtypeStruct((n, d), F32),
        grid=(n // tm,),
        in_specs=in_specs,
        out_specs=pl.BlockSpec((tm, d), row),
        compiler_params=pltpu.CompilerParams(dimension_semantics=("arbitrary",),
                                             vmem_limit_bytes=VMEM_LIMIT),
        name="out_ffn",
    )(*args)


def kernel(x_prompt, x_sample, cache_sb_k, cache_sb_v, cache_band_k, cache_band_v, norm_mix_g, w_in,
           rel_bias, norm_sb_g, norm_band_g, w_out, norm_ffn_g, w_up, w_down, norm_final_g):
    depth = w_in.shape[0]
    bsz, seq, d_model = x_prompt.shape
    dbsz, dseq, _ = x_sample.shape
    past = cache_sb_k.shape[2]
    band_rows = cache_band_k.shape[2]
    n_sb, n_bd = cache_sb_k.shape[3], cache_band_k.shape[3]
    w_sb, w_bd = n_sb * HEAD_DIM, n_bd * HEAD_DIM
    keep_p = min(PAST_CHUNKS * CHUNK, seq)
    back = PAST_CHUNKS * CHUNK
    assert w_sb == w_bd and w_in.shape[2] == 6 * w_sb
    assert seq % KEY_BLOCK == 0 and seq >= back + KEY_BLOCK and keep_p == ROW_TILE
    assert band_rows == back and past >= band_rows
    sb_qscale = LOG2E / math.sqrt(HEAD_DIM)
    bd_qscale = LOG2E / math.sqrt(HEAD_DIM)
    time_minor = lambda c: jnp.transpose(c, (0, 1, 3, 4, 2))

    h_p, h_s = x_prompt, x_sample
    outs = [[] for _ in range(8)]
    for l in range(depth):
        w_in_l = w_in[l].astype(BF16)
        w_out_l, w_up_l, w_dn_l = (w[l].astype(BF16) for w in (w_out, w_up, w_down))
        last = l == depth - 1
        tail = dict(g_sb=norm_sb_g[l], g_bd=norm_band_g[l], w_out=w_out_l, g_ffn=norm_ffn_g[l],
                    w_up=w_up_l, w_dn=w_dn_l, g_fin=norm_final_g, apply_final=last)

        n = bsz * seq
        (qsb, ksb, vsb, qbd, kbd, vbd, ksb32, vsb32, kbd32, vbd32) = _norm_proj(
            h_p.reshape(n, d_model), norm_mix_g[l], w_in_l, tail_period=seq // ROW_TILE,
            sb_qscale=sb_qscale, bd_qscale=bd_qscale)
        r3 = lambda a: a.reshape(bsz, -1, a.shape[-1])
        qsb, ksb, vsb, qbd, kbd, vbd = map(r3, (qsb, ksb, vsb, qbd, kbd, vbd))
        o_sb = _sb_attn_prompt(qsb, ksb, vsb, n_sub=SB_SUB_BLOCKS)
        kw = back + KEY_BLOCK
        o_bd = _band_attn_prompt(qbd, kbd, vbd, _prompt_band_bias(rel_bias[l], KEY_BLOCK, kw, back),
                                 bq=KEY_BLOCK, n_sub=BAND_SUB_BLOCKS, kw=kw, back=back)
        h_p = _out_ffn(h_p.reshape(n, d_model), o_sb.reshape(n, w_sb), o_bd.reshape(n, w_bd),
                       **tail).reshape(bsz, seq, d_model)
        outs[0].append(ksb32.reshape(bsz, seq, n_sb, HEAD_DIM))
        outs[1].append(vsb32.reshape(bsz, seq, n_sb, HEAD_DIM))
        outs[2].append(kbd32.reshape(bsz, keep_p, n_bd, HEAD_DIM))
        outs[3].append(vbd32.reshape(bsz, keep_p, n_bd, HEAD_DIM))

        n = dbsz * dseq
        (qsb, ksb, vsb, qbd, kbd, vbd, ksb32, vsb32, kbd32, vbd32) = _norm_proj(
            h_s.reshape(n, d_model), norm_mix_g[l], w_in_l, tail_period=None,
            sb_qscale=sb_qscale, bd_qscale=bd_qscale)
        r3 = lambda a: a.reshape(dbsz, -1, a.shape[-1])
        qsb, ksb, vsb, qbd, kbd, vbd = map(r3, (qsb, ksb, vsb, qbd, kbd, vbd))
        o_sb = _sb_attn_sample(qsb, ksb, vsb, time_minor(cache_sb_k), time_minor(cache_sb_v), l)
        new_rows = -(-dseq // LANES) * LANES
        k_pos = past - band_rows + jnp.arange(band_rows + new_rows)
        valid = (k_pos >= 0) & (k_pos < past + dseq)
        bias = _band_bias(rel_bias[l], band_rows, jnp.broadcast_to(valid[None, :], (dseq, valid.size)))
        pad = lambda a: jnp.pad(a, ((0, 0), (0, new_rows - dseq), (0, 0)))
        o_bd = _band_attn_sample(qbd, pad(kbd), pad(vbd), time_minor(cache_band_k),
                                 time_minor(cache_band_v), bias, l)
        h_s = _out_ffn(h_s.reshape(n, d_model), o_sb.reshape(n, w_sb), o_bd.reshape(n, w_bd),
                       **tail).reshape(dbsz, dseq, d_model)
        outs[4].append(ksb32.reshape(dbsz, dseq, n_sb, HEAD_DIM))
        outs[5].append(vsb32.reshape(dbsz, dseq, n_sb, HEAD_DIM))
        outs[6].append(kbd32.reshape(dbsz, dseq, n_bd, HEAD_DIM))
        outs[7].append(vbd32.reshape(dbsz, dseq, n_bd, HEAD_DIM))

    return (h_p, h_s) + tuple(jnp.stack(o) for o in outs)
```

```python
import functools
import math

import jax
import jax.numpy as jnp
from jax import lax
from jax.experimental import pallas as pl
from jax.experimental.pallas import tpu as pltpu

F32 = jnp.float32
BF16 = jnp.bfloat16

HEAD_DIM = 64
CHUNK = 64
PAST_CHUNKS = 8
REL_CLIP = 2 * CHUNK
EPS = 1e-6
NEG_INF = -1e30
LOG2E = 1.4426950408889634
SB_CUTOFF = 150.0
SB_EXP_CLAMP = 64.0

LANES = 128
KEY_BLOCK = 256
ROW_TILE = 512
BAND_SUB_BLOCKS = 32
SB_SUB_BLOCKS = 16
VMEM_LIMIT = 56 * 1024 * 1024

_NT = (((1,), (1,)), ((), ()))


def _rms(x, g):
    return x * lax.rsqrt(jnp.mean(x * x, axis=-1, keepdims=True) + EPS) * g


def _head_masks(shape):
    lane = lax.broadcasted_iota(jnp.int32, shape, 1)
    return lane < HEAD_DIM, lane >= HEAD_DIM


def _nt(a, b):
    return lax.dot_general(a, b, _NT, preferred_element_type=F32)


def _nn(a, b):
    return jnp.dot(a, b, preferred_element_type=F32)


def _norm_proj_kernel(x_ref, g_ref, w_ref, qsb_ref, ksb_ref, vsb_ref, qbd_ref, kbd_ref, vbd_ref,
                      ksb32_ref, vsb32_ref, kbd32_ref, vbd32_ref, *, width, tail_period, sb_qscale,
                      bd_qscale):
    xn = _rms(x_ref[...], g_ref[...]).astype(BF16)

    def proj(c):
        return _nn(xn, w_ref[:, c * width:(c + 1) * width])

    def split_heads(p):
        return p.reshape(p.shape[0], -1, HEAD_DIM)

    qsb_ref[...] = (proj(0) * sb_qscale).astype(BF16)
    p = proj(1)
    ksb_ref[...] = p.astype(BF16)
    ksb32_ref[...] = split_heads(p)
    p = proj(2)
    vsb_ref[...] = p.astype(BF16)
    vsb32_ref[...] = split_heads(p)
    qbd_ref[...] = (proj(3) * bd_qscale).astype(BF16)
    pk = proj(4)
    kbd_ref[...] = pk.astype(BF16)
    pv = proj(5)
    vbd_ref[...] = pv.astype(BF16)

    def write_band32():
        kbd32_ref[...] = split_heads(pk)
        vbd32_ref[...] = split_heads(pv)

    if tail_period is None:
        write_band32()
    else:
        pl.when(pl.program_id(0) % tail_period == tail_period - 1)(write_band32)


def _norm_proj(x2, g, w_bf16, *, tail_period, sb_qscale, bd_qscale):
    n, d = x2.shape
    width = w_bf16.shape[1] // 6
    tm = min(ROW_TILE, n)
    assert n % tm == 0
    nt = n // tm
    row = lambda i: (i, 0)
    const = lambda i: (0, 0)
    heads = width // HEAD_DIM
    if tail_period is None:
        band_rows, band_map = n, (lambda i: (i, 0, 0))
    else:
        assert nt % tail_period == 0
        band_rows, band_map = (nt // tail_period) * tm, (lambda i: (i // tail_period, 0, 0))
    blk = pl.BlockSpec((tm, width), row)
    blk32 = lambda m: pl.BlockSpec((tm, heads, HEAD_DIM), m)
    out_shape = ([jax.ShapeDtypeStruct((n, width), BF16)] * 6
                 + [jax.ShapeDtypeStruct((n, heads, HEAD_DIM), F32)] * 2
                 + [jax.ShapeDtypeStruct((band_rows, heads, HEAD_DIM), F32)] * 2)
    return pl.pallas_call(
        functools.partial(_norm_proj_kernel, width=width, tail_period=tail_period,
                          sb_qscale=sb_qscale, bd_qscale=bd_qscale),
        out_shape=out_shape,
        grid=(nt,),
        in_specs=[pl.BlockSpec((tm, d), row), pl.BlockSpec((1, d), const),
                  pl.BlockSpec(w_bf16.shape, const)],
        out_specs=[blk] * 6 + [blk32(lambda i: (i, 0, 0))] * 2 + [blk32(band_map)] * 2,
        compiler_params=pltpu.CompilerParams(dimension_semantics=("arbitrary",),
                                             vmem_limit_bytes=VMEM_LIMIT),
        name="norm_proj",
    )(x2, g.reshape(1, d), w_bf16)


def _sb_softplus2(z):
    return jnp.maximum(z, jnp.log2(1.0 + jnp.exp2(jnp.minimum(z, SB_EXP_CLAMP))))


def _run_skewed(tasks):
    depth = max(len(t) for t in tasks)
    vals = [None] * len(tasks)
    for i in range(len(tasks) + depth - 1):
        for k in range(depth):
            j = i - k
            if 0 <= j < len(tasks) and k < len(tasks[j]):
                vals[j] = tasks[j][k](vals[j])
    return vals


def _below_diagonal(fn, *tiles):
    h, w = tiles[0].shape[0] // 2, tiles[0].shape[1] // 2
    top = fn(*[t[:h, :w] for t in tiles])
    bottom = fn(*[t[h:, :] for t in tiles])
    return jnp.concatenate([jnp.concatenate([top, jnp.zeros_like(top)], axis=1), bottom], axis=0)


def _sb_tasks(chains, tri, swept):
    def task(score_fn, pv_fn, bias, carry_fn):
        square = bias is not None and bias.shape[0] == bias.shape[1]
        on = _below_diagonal if square else (lambda fn, *tiles: fn(*tiles))

        def scores(_):
            return score_fn() if bias is None else on(jnp.add, score_fn(), bias)

        def suffix(z):
            sp = on(_sb_softplus2, z)
            swept.append(jnp.sum(sp, axis=-1, keepdims=True))
            return on(jnp.subtract, z, sp), _nn(sp.astype(BF16), tri)

        def weigh(stage):
            carry = carry_fn(swept)
            if carry is None:
                w = on(lambda ls, excl: jnp.exp2(ls - excl).astype(BF16), *stage)
            else:
                w = jnp.exp2(stage[0] - stage[1] - carry).astype(BF16)
            return pv_fn(w)

        return [scores, suffix, weigh]

    return [task(*c) for c in chains]


def _sb_sweep(streams, n_past, tri, causal, store):
    ids = range(len(streams))
    big = lambda ok: jnp.where(ok, 0.0, -NEG_INF).astype(F32)

    chains = []
    no_past = [big(n > 0) for n in n_past]
    for s in ids:
        diag, past = streams[s]
        chains.append((*diag, causal, lambda swept: None))
        chains.append((*past(jnp.maximum(n_past[s] - 1, 0)), None,
                       lambda swept, s=s: swept[2 * s] + no_past[s]))
    swept = []
    pv = _run_skewed(_sb_tasks(chains, tri, swept))
    store([pv[2 * s] + pv[2 * s + 1] for s in ids], False)
    carries = [swept[2 * s] + no_past[s] + swept[2 * s + 1] for s in ids]

    def least(cs):
        return jnp.min(functools.reduce(jnp.minimum, cs))

    most_past = functools.reduce(jnp.maximum, n_past)

    def cond(state):
        return jnp.logical_and(state[0] < most_past - 1, state[1] < SB_CUTOFF)

    def body(state):
        t, cs = state[0], state[2:]
        chains, start = [], []
        for s in ids:
            j = n_past[s] - 2 - t
            start.append(cs[s] + big(j >= 0))
            chains.append((*streams[s][1](jnp.maximum(j, 0)), None, lambda swept, c=start[-1]: c))
        swept = []
        store(_run_skewed(_sb_tasks(chains, tri, swept)), True)
        new = [c + w for c, w in zip(start, swept)]
        return (t + 1, least(new), *new)

    lax.while_loop(cond, body, (0, least(carries), *carries))


def _sb_sample_kernel(q_ref, kn_ref, vn_ref, ck_ref, cv_ref, tri_ref, causal_ref, o_ref, *, n_past):
    heads = ck_ref.shape[0]
    streams = []
    for h in range(heads):
        hl = slice(h * HEAD_DIM, (h + 1) * HEAD_DIM)
        qh = q_ref[:, hl]

        def past(j, h=h, qh=qh):
            blk = pl.ds(pl.multiple_of(j * KEY_BLOCK, KEY_BLOCK), KEY_BLOCK)
            return ((lambda: _nn(qh, ck_ref[h, :, blk].astype(BF16))),
                    (lambda w: _nt(w, cv_ref[h, :, blk].astype(BF16))))

        diag = ((lambda qh=qh, hl=hl: _nt(qh, kn_ref[:, hl])), (lambda w, hl=hl: _nn(w, vn_ref[:, hl])))
        streams.append((diag, past))

    def store(pv, accumulate):
        for h in range(heads):
            hl = slice(h * HEAD_DIM, (h + 1) * HEAD_DIM)
            o_ref[:, hl] = o_ref[:, hl] + pv[h] if accumulate else pv[h]

    _sb_sweep(streams, [n_past] * heads, tri_ref[...], causal_ref[...], store)


def _sb_constants(bq):
    row = lax.broadcasted_iota(jnp.int32, (KEY_BLOCK, KEY_BLOCK), 0)
    col = lax.broadcasted_iota(jnp.int32, (KEY_BLOCK, KEY_BLOCK), 1)
    tri = (row > col).astype(BF16)
    causal = jnp.where(col < row, 0.0, NEG_INF).astype(F32)[:bq]
    return tri, causal


def _cache_spec(cache_t, layer):
    return pl.BlockSpec((None, None) + cache_t.shape[2:], lambda bi: (layer, bi, 0, 0, 0))


def _sb_attn_sample(q, k_new, v_new, cache_kt, cache_vt, layer):
    b, t, w = q.shape
    heads, past = cache_kt.shape[2], cache_kt.shape[4]
    assert t <= KEY_BLOCK and past % KEY_BLOCK == 0 and heads * HEAD_DIM == w
    tri, causal = _sb_constants(t)
    pad = lambda a: jnp.pad(a, ((0, 0), (0, KEY_BLOCK - t), (0, 0)))
    tok = lambda rows: pl.BlockSpec((None, rows, w), lambda bi: (bi, 0, 0))
    const = lambda bi: (0, 0)
    return pl.pallas_call(
        functools.partial(_sb_sample_kernel, n_past=past // KEY_BLOCK),
        out_shape=jax.ShapeDtypeStruct((b, t, w), F32),
        grid=(b,),
        in_specs=[tok(t), tok(KEY_BLOCK), tok(KEY_BLOCK), _cache_spec(cache_kt, layer),
                  _cache_spec(cache_vt, layer), pl.BlockSpec((KEY_BLOCK, KEY_BLOCK), const),
                  pl.BlockSpec((t, KEY_BLOCK), const)],
        out_specs=tok(t),
        compiler_params=pltpu.CompilerParams(dimension_semantics=("arbitrary",),
                                             vmem_limit_bytes=VMEM_LIMIT),
        name="sb_attn_sample",
    )(q, pad(k_new), pad(v_new), cache_kt, cache_vt, tri, causal)


def _band_tasks(units):
    def task(score_fn, pv_fn):
        def attend(s):
            p = jnp.exp2(s - jnp.max(s, axis=-1, keepdims=True))
            den = jnp.sum(p, axis=-1, keepdims=True)
            return pv_fn(p.astype(BF16)) / den

        return [lambda _: score_fn(), lambda s: s, attend]

    return [task(*u) for u in units]


def _band_sample_kernel(q_ref, kn_ref, vn_ref, ck_ref, cv_ref, bias_ref, o_ref):
    heads, _, rows = ck_ref.shape

    def unit(h):
        hl = slice(h * HEAD_DIM, (h + 1) * HEAD_DIM)

        def scores():
            qh = q_ref[:, hl]
            s = jnp.concatenate([_nn(qh, ck_ref[h].astype(BF16)), _nt(qh, kn_ref[:, hl])], axis=1)
            return s + bias_ref[h]

        def pv(p):
            return _nt(p[:, :rows], cv_ref[h].astype(BF16)) + _nn(p[:, rows:], vn_ref[:, hl])

        return scores, pv

    outs = _run_skewed(_band_tasks([unit(h) for h in range(heads)]))
    for h in range(heads):
        o_ref[:, h * HEAD_DIM:(h + 1) * HEAD_DIM] = outs[h]


def _band_attn_sample(q, k_new, v_new, cache_kt, cache_vt, bias, layer):
    b, t, w = q.shape
    tn = k_new.shape[1]
    heads, rows = cache_kt.shape[2], cache_kt.shape[4]
    assert heads * HEAD_DIM == w and bias.shape == (heads, t, rows + tn)
    tok = lambda r: pl.BlockSpec((None, r, w), lambda bi: (bi, 0, 0))
    return pl.pallas_call(
        _band_sample_kernel,
        out_shape=jax.ShapeDtypeStruct((b, t, w), F32),
        grid=(b,),
        in_specs=[tok(t), tok(tn), tok(tn), _cache_spec(cache_kt, layer),
                  _cache_spec(cache_vt, layer), pl.BlockSpec(bias.shape, lambda bi: (0, 0, 0))],
        out_specs=tok(t),
        compiler_params=pltpu.CompilerParams(dimension_semantics=("arbitrary",),
                                             vmem_limit_bytes=VMEM_LIMIT),
        name="band_attn_sample",
    )(q, k_new, v_new, cache_kt, cache_vt, bias)


def _sb_prompt_kernel(q_ref, kd_ref, vd_ref, kp_ref, vp_ref, tri_ref, causal_ref, o_ref, *, n_sub):
    rows = lambda u: slice(u * KEY_BLOCK, (u + 1) * KEY_BLOCK)
    head_lanes = _head_masks((KEY_BLOCK, LANES))
    streams, n_past = [], []
    for u in range(n_sub):
        q = q_ref[rows(u), :]
        for m in head_lanes:
            qh = jnp.where(m, q, jnp.zeros_like(q))

            def past(j, qh=qh):
                blk = pl.ds(pl.multiple_of(j * KEY_BLOCK, KEY_BLOCK), KEY_BLOCK)
                return (lambda: _nt(qh, kp_ref[blk, :])), (lambda w: _nn(w, vp_ref[blk, :]))

            diag = ((lambda qh=qh, u=u: _nt(qh, kd_ref[rows(u), :])),
                    (lambda w, u=u: _nn(w, vd_ref[rows(u), :])))
            streams.append((diag, past))
            n_past.append(pl.program_id(2) * n_sub + u)

    def store(pv, accumulate):
        for u in range(n_sub):
            val = jnp.where(head_lanes[0], pv[2 * u], pv[2 * u + 1])
            o_ref[rows(u), :] = o_ref[rows(u), :] + val if accumulate else val

    _sb_sweep(streams, n_past, tri_ref[...], causal_ref[...], store)


def _sb_attn_prompt(q, k, v, *, n_sub):
    b, t, w = q.shape
    rows = KEY_BLOCK * n_sub
    assert t % rows == 0 and w % LANES == 0
    tri, causal = _sb_constants(KEY_BLOCK)
    qmap = lambda bi, hp, i: (bi, i, hp)
    pmap = lambda bi, hp, i: (bi, 0, hp)
    const = lambda bi, hp, i: (0, 0)
    return pl.pallas_call(
        functools.partial(_sb_prompt_kernel, n_sub=n_sub),
        out_shape=jax.ShapeDtypeStruct((b, t, w), F32),
        grid=(b, w // LANES, t // rows),
        in_specs=[pl.BlockSpec((None, rows, LANES), qmap),
                  pl.BlockSpec((None, rows, LANES), qmap),
                  pl.BlockSpec((None, rows, LANES), qmap),
                  pl.BlockSpec((None, t, LANES), pmap),
                  pl.BlockSpec((None, t, LANES), pmap),
                  pl.BlockSpec((KEY_BLOCK, KEY_BLOCK), const),
                  pl.BlockSpec((KEY_BLOCK, KEY_BLOCK), const)],
        out_specs=pl.BlockSpec((None, rows, LANES), qmap),
        compiler_params=pltpu.CompilerParams(
            dimension_semantics=("parallel", "parallel", "arbitrary"),
            vmem_limit_bytes=VMEM_LIMIT),
        name="sb_attn",
    )(q, k, v, k, v, tri, causal)


def _band_prompt_kernel(q_ref, k_ref, v_ref, bias_ref, o_ref, *, bq, n_sub, kw, back):
    head_lanes = _head_masks((bq, LANES))

    def unit(u, h):
        first = (pl.program_id(2) * n_sub + u) * bq - back
        start = jnp.maximum(first, 0)
        win = pl.ds(pl.multiple_of(start, bq), kw)
        cols = pl.ds(pl.multiple_of(start - first, bq), kw)

        def scores():
            q = q_ref[u * bq:(u + 1) * bq, :]
            qh = jnp.where(head_lanes[h], q, jnp.zeros_like(q))
            return _nt(qh, k_ref[win, :]) + bias_ref[h, :, cols]

        return scores, lambda p: _nn(p, v_ref[win, :])

    outs = _run_skewed(_band_tasks([unit(u, h) for u in range(n_sub) for h in range(2)]))
    for u in range(n_sub):
        o_ref[u * bq:(u + 1) * bq, :] = jnp.where(head_lanes[0], outs[2 * u], outs[2 * u + 1])


def _band_attn_prompt(q, k, v, bias, *, bq, n_sub, kw, back):
    b, t, w = q.shape
    rows = bq * n_sub
    assert t % rows == 0 and w % LANES == 0 and back % bq == 0 and bias.shape[1:] == (bq, back + kw)
    return pl.pallas_call(
        functools.partial(_band_prompt_kernel, bq=bq, n_sub=n_sub, kw=kw, back=back),
        out_shape=jax.ShapeDtypeStruct((b, t, w), F32),
        grid=(b, w // LANES, t // rows),
        in_specs=[pl.BlockSpec((None, rows, LANES), lambda bi, hp, i: (bi, i, hp)),
                  pl.BlockSpec((None, t, LANES), lambda bi, hp, i: (bi, 0, hp)),
                  pl.BlockSpec((None, t, LANES), lambda bi, hp, i: (bi, 0, hp)),
                  pl.BlockSpec((2, bq, back + kw), lambda bi, hp, i: (hp, 0, 0))],
        out_specs=pl.BlockSpec((None, rows, LANES), lambda bi, hp, i: (bi, i, hp)),
        compiler_params=pltpu.CompilerParams(
            dimension_semantics=("parallel", "parallel", "arbitrary"),
            vmem_limit_bytes=VMEM_LIMIT),
        name="band_attn",
    )(q, k, v, bias)


def _band_bias(rel_bias, offset, valid):
    rows, cols = valid.shape
    w = -(-(rows + cols - 1) // LANES) * LANES + 1
    m = jnp.arange(w)
    rel = jnp.clip(offset - jnp.where(m < cols, m, m - w), -REL_CLIP, REL_CLIP) + REL_CLIP
    base = rel_bias.astype(F32)[:, rel] * LOG2E
    tiled = jnp.tile(base, (1, rows))[:, :rows * (w - 1)].reshape(-1, rows, w - 1)
    return jnp.where(valid[None], tiled[:, :, :cols], NEG_INF)


def _prompt_band_bias(rel_bias, bq, kw, back):
    qc = (back + jnp.arange(bq))[:, None] // CHUNK
    kc = jnp.arange(back + kw)[None, :] // CHUNK
    return _band_bias(rel_bias, back, (kc <= qc) & (kc >= qc - PAST_CHUNKS))


def _out_ffn_kernel(h_ref, osb_ref, obd_ref, gsb_ref, gbd_ref, wout_ref, gffn_ref, wup_ref, wdn_ref,
                    gfin_ref, y_ref, *, w_sb, ff_chunk, apply_final):
    ysb = _rms(osb_ref[...], gsb_ref[...]).astype(BF16)
    ybd = _rms(obd_ref[...], gbd_ref[...]).astype(BF16)
    h1 = h_ref[...] + _nn(ysb, wout_ref[:w_sb, :]) + _nn(ybd, wout_ref[w_sb:, :])
    xn = _rms(h1, gffn_ref[...]).astype(BF16)
    mlp = None
    for c in range(wup_ref.shape[1] // ff_chunk):
        u = _nn(xn, wup_ref[:, c * ff_chunk:(c + 1) * ff_chunk])
        a = jnp.square(jnp.maximum(u, 0.0)).astype(BF16)
        d = _nn(a, wdn_ref[c * ff_chunk:(c + 1) * ff_chunk, :])
        mlp = d if mlp is None else mlp + d
    h2 = h1 + mlp
    y_ref[...] = _rms(h2, gfin_ref[...]) if apply_final else h2


def _out_ffn(h2, osb, obd, g_sb, g_bd, w_out, g_ffn, w_up, w_dn, g_fin, *, apply_final):
    n, d = h2.shape
    w_sb, w_bd = osb.shape[1], obd.shape[1]
    tm = min(ROW_TILE, n)
    assert n % tm == 0
    row = lambda i: (i, 0)
    const = lambda i: (0, 0)
    resident = lambda a: pl.BlockSpec(a.shape, const, pipeline_mode=pl.Buffered(1))
    vec = lambda a: a.reshape(1, -1).astype(F32)
    args = (h2, osb, obd, vec(g_sb), vec(g_bd), w_out, vec(g_ffn), w_up, w_dn, vec(g_fin))
    in_specs = [pl.BlockSpec((tm, d), row), pl.BlockSpec((tm, w_sb), row),
                pl.BlockSpec((tm, w_bd), row)] + [resident(a) for a in args[3:]]
    return pl.pallas_call(
        functools.partial(_out_ffn_kernel, w_sb=w_sb, ff_chunk=min(1024, w_up.shape[1]),
                          apply_final=apply_final),
        out_shape=jax.ShapeDtypeStruct((n, d), F32),
        grid=(n // tm,),
        in_specs=in_specs,
        out_specs=pl.BlockSpec((tm, d), row),
        compiler_params=pltpu.CompilerParams(dimension_semantics=("arbitrary",),
                                             vmem_limit_bytes=VMEM_LIMIT),
        name="out_ffn",
    )(*args)


def kernel(x_prompt, x_sample, cache_sb_k, cache_sb_v, cache_band_k, cache_band_v, norm_mix_g, w_in,
           rel_bias, norm_sb_g, norm_band_g, w_out, norm_ffn_g, w_up, w_down, norm_final_g):
    depth = w_in.shape[0]
    bsz, seq, d_model = x_prompt.shape
    dbsz, dseq, _ = x_sample.shape
    past = cache_sb_k.shape[2]
    band_rows = cache_band_k.shape[2]
    n_sb, n_bd = cache_sb_k.shape[3], cache_band_k.shape[3]
    w_sb, w_bd = n_sb * HEAD_DIM, n_bd * HEAD_DIM
    keep_p = min(PAST_CHUNKS * CHUNK, seq)
    back = PAST_CHUNKS * CHUNK
    assert w_sb == w_bd and w_in.shape[2] == 6 * w_sb
    assert seq % KEY_BLOCK == 0 and seq >= back + KEY_BLOCK and keep_p == ROW_TILE
    assert band_rows == back and past >= band_rows
    sb_qscale = LOG2E / math.sqrt(HEAD_DIM)
    bd_qscale = LOG2E / math.sqrt(HEAD_DIM)
    time_minor = lambda c: jnp.transpose(c, (0, 1, 3, 4, 2))

    h_p, h_s = x_prompt, x_sample
    outs = [[] for _ in range(8)]
    for l in range(depth):
        w_in_l = w_in[l].astype(BF16)
        w_out_l, w_up_l, w_dn_l = (w[l].astype(BF16) for w in (w_out, w_up, w_down))
        last = l == depth - 1
        tail = dict(g_sb=norm_sb_g[l], g_bd=norm_band_g[l], w_out=w_out_l, g_ffn=norm_ffn_g[l],
                    w_up=w_up_l, w_dn=w_dn_l, g_fin=norm_final_g, apply_final=last)

        n = bsz * seq
        (qsb, ksb, vsb, qbd, kbd, vbd, ksb32, vsb32, kbd32, vbd32) = _norm_proj(
            h_p.reshape(n, d_model), norm_mix_g[l], w_in_l, tail_period=seq // ROW_TILE,
            sb_qscale=sb_qscale, bd_qscale=bd_qscale)
        r3 = lambda a: a.reshape(bsz, -1, a.shape[-1])
        qsb, ksb, vsb, qbd, kbd, vbd = map(r3, (qsb, ksb, vsb, qbd, kbd, vbd))
        o_sb = _sb_attn_prompt(qsb, ksb, vsb, n_sub=SB_SUB_BLOCKS)
        kw = back + KEY_BLOCK
        o_bd = _band_attn_prompt(qbd, kbd, vbd, _prompt_band_bias(rel_bias[l], KEY_BLOCK, kw, back),
                                 bq=KEY_BLOCK, n_sub=BAND_SUB_BLOCKS, kw=kw, back=back)
        h_p = _out_ffn(h_p.reshape(n, d_model), o_sb.reshape(n, w_sb), o_bd.reshape(n, w_bd),
                       **tail).reshape(bsz, seq, d_model)
        outs[0].append(ksb32.reshape(bsz, seq, n_sb, HEAD_DIM))
        outs[1].append(vsb32.reshape(bsz, seq, n_sb, HEAD_DIM))
        outs[2].append(kbd32.reshape(bsz, keep_p, n_bd, HEAD_DIM))
        outs[3].append(vbd32.reshape(bsz, keep_p, n_bd, HEAD_DIM))

        n = dbsz * dseq
        (qsb, ksb, vsb, qbd, kbd, vbd, ksb32, vsb32, kbd32, vbd32) = _norm_proj(
            h_s.reshape(n, d_model), norm_mix_g[l], w_in_l, tail_period=None,
            sb_qscale=sb_qscale, bd_qscale=bd_qscale)
        r3 = lambda a: a.reshape(dbsz, -1, a.shape[-1])
        qsb, ksb, vsb, qbd, kbd, vbd = map(r3, (qsb, ksb, vsb, qbd, kbd, vbd))
        o_sb = _sb_attn_sample(qsb, ksb, vsb, time_minor(cache_sb_k), time_minor(cache_sb_v), l)
        new_rows = -(-dseq // LANES) * LANES
        k_pos = past - band_rows + jnp.arange(band_rows + new_rows)
        valid = (k_pos >= 0) & (k_pos < past + dseq)
        bias = _band_bias(rel_bias[l], band_rows, jnp.broadcast_to(valid[None, :], (dseq, valid.size)))
        pad = lambda a: jnp.pad(a, ((0, 0), (0, new_rows - dseq), (0, 0)))
        o_bd = _band_attn_sample(qbd, pad(kbd), pad(vbd), time_minor(cache_band_k),
                                 time_minor(cache_band_v), bias, l)
        h_s = _out_ffn(h_s.reshape(n, d_model), o_sb.reshape(n, w_sb), o_bd.reshape(n, w_bd),
                       **tail).reshape(dbsz, dseq, d_model)
        outs[4].append(ksb32.reshape(dbsz, dseq, n_sb, HEAD_DIM))
        outs[5].append(vsb32.reshape(dbsz, dseq, n_sb, HEAD_DIM))
        outs[6].append(kbd32.reshape(dbsz, dseq, n_bd, HEAD_DIM))
        outs[7].append(vbd32.reshape(dbsz, dseq, n_bd, HEAD_DIM))

    return (h_p, h_s) + tuple(jnp.stack(o) for o in outs)
```

```python
import functools
import math

import jax
import jax.numpy as jnp
from jax import lax
from jax.experimental import pallas as pl
from jax.experimental.pallas import tpu as pltpu

F32 = jnp.float32
BF16 = jnp.bfloat16

HEAD_DIM = 64
CHUNK = 64
PAST_CHUNKS = 8
REL_CLIP = 2 * CHUNK
EPS = 1e-6
NEG_INF = -1e30
LOG2E = 1.4426950408889634
SB_CUTOFF = 150.0
SB_EXP_CLAMP = 64.0

LANES = 128
KEY_BLOCK = 256
ROW_TILE = 512
BAND_SUB_BLOCKS = 32
SB_SUB_BLOCKS = 16
VMEM_LIMIT = 56 * 1024 * 1024

_NT = (((1,), (1,)), ((), ()))


def _rms(x, g):
    return x * lax.rsqrt(jnp.mean(x * x, axis=-1, keepdims=True) + EPS) * g


def _head_masks(shape):
    lane = lax.broadcasted_iota(jnp.int32, shape, 1)
    return lane < HEAD_DIM, lane >= HEAD_DIM


def _nt(a, b):
    return lax.dot_general(a, b, _NT, preferred_element_type=F32)


def _nn(a, b):
    return jnp.dot(a, b, preferred_element_type=F32)


def _norm_proj_kernel(x_ref, g_ref, w_ref, qsb_ref, ksb_ref, vsb_ref, qbd_ref, kbd_ref, vbd_ref,
                      ksb32_ref, vsb32_ref, kbd32_ref, vbd32_ref, *, width, tail_period, sb_qscale,
                      bd_qscale):
    xn = _rms(x_ref[...], g_ref[...]).astype(BF16)

    def proj(c):
        return _nn(xn, w_ref[:, c * width:(c + 1) * width])

    def split_heads(p):
        return p.reshape(p.shape[0], -1, HEAD_DIM)

    qsb_ref[...] = (proj(0) * sb_qscale).astype(BF16)
    p = proj(1)
    ksb_ref[...] = p.astype(BF16)
    ksb32_ref[...] = split_heads(p)
    p = proj(2)
    vsb_ref[...] = p.astype(BF16)
    vsb32_ref[...] = split_heads(p)
    qbd_ref[...] = (proj(3) * bd_qscale).astype(BF16)
    pk = proj(4)
    kbd_ref[...] = pk.astype(BF16)
    pv = proj(5)
    vbd_ref[...] = pv.astype(BF16)

    def write_band32():
        kbd32_ref[...] = split_heads(pk)
        vbd32_ref[...] = split_heads(pv)

    if tail_period is None:
        write_band32()
    else:
        pl.when(pl.program_id(0) % tail_period == tail_period - 1)(write_band32)


def _norm_proj(x2, g, w_bf16, *, tail_period, sb_qscale, bd_qscale):
    n, d = x2.shape
    width = w_bf16.shape[1] // 6
    tm = min(ROW_TILE, n)
    assert n % tm == 0
    nt = n // tm
    row = lambda i: (i, 0)
    const = lambda i: (0, 0)
    heads = width // HEAD_DIM
    if tail_period is None:
        band_rows, band_map = n, (lambda i: (i, 0, 0))
    else:
        assert nt % tail_period == 0
        band_rows, band_map = (nt // tail_period) * tm, (lambda i: (i // tail_period, 0, 0))
    blk = pl.BlockSpec((tm, width), row)
    blk32 = lambda m: pl.BlockSpec((tm, heads, HEAD_DIM), m)
    out_shape = ([jax.ShapeDtypeStruct((n, width), BF16)] * 6
                 + [jax.ShapeDtypeStruct((n, heads, HEAD_DIM), F32)] * 2
                 + [jax.ShapeDtypeStruct((band_rows, heads, HEAD_DIM), F32)] * 2)
    return pl.pallas_call(
        functools.partial(_norm_proj_kernel, width=width, tail_period=tail_period,
                          sb_qscale=sb_qscale, bd_qscale=bd_qscale),
        out_shape=out_shape,
        grid=(nt,),
        in_specs=[pl.BlockSpec((tm, d), row), pl.BlockSpec((1, d), const),
                  pl.BlockSpec(w_bf16.shape, const)],
        out_specs=[blk] * 6 + [blk32(lambda i: (i, 0, 0))] * 2 + [blk32(band_map)] * 2,
        compiler_params=pltpu.CompilerParams(dimension_semantics=("arbitrary",),
                                             vmem_limit_bytes=VMEM_LIMIT),
        name="norm_proj",
    )(x2, g.reshape(1, d), w_bf16)


def _sb_softplus2(z):
    return jnp.maximum(z, jnp.log2(1.0 + jnp.exp2(jnp.minimum(z, SB_EXP_CLAMP))))


def _run_skewed(tasks):
    depth = max(len(t) for t in tasks)
    vals = [None] * len(tasks)
    for i in range(len(tasks) + depth - 1):
        for k in range(depth):
            j = i - k
            if 0 <= j < len(tasks) and k < len(tasks[j]):
                vals[j] = tasks[j][k](vals[j])
    return vals


def _below_diagonal(fn, *tiles):
    h, w = tiles[0].shape[0] // 2, tiles[0].shape[1] // 2
    top = fn(*[t[:h, :w] for t in tiles])
    bottom = fn(*[t[h:, :] for t in tiles])
    return jnp.concatenate([jnp.concatenate([top, jnp.zeros_like(top)], axis=1), bottom], axis=0)


def _sb_tasks(chains, tri, swept):
    def task(score_fn, pv_fn, bias, carry_fn):
        square = bias is not None and bias.shape[0] == bias.shape[1]
        on = _below_diagonal if square else (lambda fn, *tiles: fn(*tiles))

        def scores(_):
            return score_fn() if bias is None else on(jnp.add, score_fn(), bias)

        def suffix(z):
            sp = on(_sb_softplus2, z)
            swept.append(jnp.sum(sp, axis=-1, keepdims=True))
            return on(jnp.subtract, z, sp), _nn(sp.astype(BF16), tri)

        def weigh(stage):
            carry = carry_fn(swept)
            if carry is None:
                w = on(lambda ls, excl: jnp.exp2(ls - excl).astype(BF16), *stage)
            else:
                w = jnp.exp2(stage[0] - stage[1] - carry).astype(BF16)
            return pv_fn(w)

        return [scores, suffix, weigh]

    return [task(*c) for c in chains]


def _sb_sweep(streams, n_past, tri, causal, store):
    ids = range(len(streams))
    big = lambda ok: jnp.where(ok, 0.0, -NEG_INF).astype(F32)

    chains = []
    no_past = [big(n > 0) for n in n_past]
    for s in ids:
        diag, past = streams[s]
        chains.append((*diag, causal, lambda swept: None))
        chains.append((*past(jnp.maximum(n_past[s] - 1, 0)), None,
                       lambda swept, s=s: swept[2 * s] + no_past[s]))
    swept = []
    pv = _run_skewed(_sb_tasks(chains, tri, swept))
    store([pv[2 * s] + pv[2 * s + 1] for s in ids], False)
    carries = [swept[2 * s] + no_past[s] + swept[2 * s + 1] for s in ids]

    def least(cs):
        return jnp.min(functools.reduce(jnp.minimum, cs))

    most_past = functools.reduce(jnp.maximum, n_past)

    def cond(state):
        return jnp.logical_and(state[0] < most_past - 1, state[1] < SB_CUTOFF)

    def body(state):
        t, cs = state[0], state[2:]
        chains, start = [], []
        for s in ids:
            j = n_past[s] - 2 - t
            start.append(cs[s] + big(j >= 0))
            chains.append((*streams[s][1](jnp.maximum(j, 0)), None, lambda swept, c=start[-1]: c))
        swept = []
        store(_run_skewed(_sb_tasks(chains, tri, swept)), True)
        new = [c + w for c, w in zip(start, swept)]
        return (t + 1, least(new), *new)

    lax.while_loop(cond, body, (0, least(carries), *carries))


def _sb_sample_kernel(q_ref, kn_ref, vn_ref, ck_ref, cv_ref, tri_ref, causal_ref, o_ref, *, n_past):
    heads = ck_ref.shape[0]
    streams = []
    for h in range(heads):
        hl = slice(h * HEAD_DIM, (h + 1) * HEAD_DIM)
        qh = q_ref[:, hl]

        def past(j, h=h, qh=qh):
            blk = pl.ds(pl.multiple_of(j * KEY_BLOCK, KEY_BLOCK), KEY_BLOCK)
            return ((lambda: _nn(qh, ck_ref[h, :, blk].astype(BF16))),
                    (lambda w: _nt(w, cv_ref[h, :, blk].astype(BF16))))

        diag = ((lambda qh=qh, hl=hl: _nt(qh, kn_ref[:, hl])), (lambda w, hl=hl: _nn(w, vn_ref[:, hl])))
        streams.append((diag, past))

    def store(pv, accumulate):
        for h in range(heads):
            hl = slice(h * HEAD_DIM, (h + 1) * HEAD_DIM)
            o_ref[:, hl] = o_ref[:, hl] + pv[h] if accumulate else pv[h]

    _sb_sweep(streams, [n_past] * heads, tri_ref[...], causal_ref[...], store)


def _sb_constants(bq):
    row = lax.broadcasted_iota(jnp.int32, (KEY_BLOCK, KEY_BLOCK), 0)
    col = lax.broadcasted_iota(jnp.int32, (KEY_BLOCK, KEY_BLOCK), 1)
    tri = (row > col).astype(BF16)
    causal = jnp.where(col < row, 0.0, NEG_INF).astype(F32)[:bq]
    return tri, causal


def _cache_spec(cache_t, layer):
    return pl.BlockSpec((None, None) + cache_t.shape[2:], lambda bi: (layer, bi, 0, 0, 0))


def _sb_attn_sample(q, k_new, v_new, cache_kt, cache_vt, layer):
    b, t, w = q.shape
    heads, past = cache_kt.shape[2], cache_kt.shape[4]
    assert t <= KEY_BLOCK and past % KEY_BLOCK == 0 and heads * HEAD_DIM == w
    tri, causal = _sb_constants(t)
    pad = lambda a: jnp.pad(a, ((0, 0), (0, KEY_BLOCK - t), (0, 0)))
    tok = lambda rows: pl.BlockSpec((None, rows, w), lambda bi: (bi, 0, 0))
    const = lambda bi: (0, 0)
    return pl.pallas_call(
        functools.partial(_sb_sample_kernel, n_past=past // KEY_BLOCK),
        out_shape=jax.ShapeDtypeStruct((b, t, w), F32),
        grid=(b,),
        in_specs=[tok(t), tok(KEY_BLOCK), tok(KEY_BLOCK), _cache_spec(cache_kt, layer),
                  _cache_spec(cache_vt, layer), pl.BlockSpec((KEY_BLOCK, KEY_BLOCK), const),
                  pl.BlockSpec((t, KEY_BLOCK), const)],
        out_specs=tok(t),
        compiler_params=pltpu.CompilerParams(dimension_semantics=("arbitrary",),
                                             vmem_limit_bytes=VMEM_LIMIT),
        name="sb_attn_sample",
    )(q, pad(k_new), pad(v_new), cache_kt, cache_vt, tri, causal)


def _band_tasks(units):
    def task(score_fn, pv_fn):
        def attend(s):
            p = jnp.exp2(s - jnp.max(s, axis=-1, keepdims=True))
            den = jnp.sum(p, axis=-1, keepdims=True)
            return pv_fn(p.astype(BF16)) / den

        return [lambda _: score_fn(), lambda s: s, attend]

    return [task(*u) for u in units]


def _band_sample_kernel(q_ref, kn_ref, vn_ref, ck_ref, cv_ref, bias_ref, o_ref):
    heads, _, rows = ck_ref.shape

    def unit(h):
        hl = slice(h * HEAD_DIM, (h + 1) * HEAD_DIM)

        def scores():
            qh = q_ref[:, hl]
            s = jnp.concatenate([_nn(qh, ck_ref[h].astype(BF16)), _nt(qh, kn_ref[:, hl])], axis=1)
            return s + bias_ref[h]

        def pv(p):
            return _nt(p[:, :rows], cv_ref[h].astype(BF16)) + _nn(p[:, rows:], vn_ref[:, hl])

        return scores, pv

    outs = _run_skewed(_band_tasks([unit(h) for h in range(heads)]))
    for h in range(heads):
        o_ref[:, h * HEAD_DIM:(h + 1) * HEAD_DIM] = outs[h]


def _band_attn_sample(q, k_new, v_new, cache_kt, cache_vt, bias, layer):
    b, t, w = q.shape
    tn = k_new.shape[1]
    heads, rows = cache_kt.shape[2], cache_kt.shape[4]
    assert heads * HEAD_DIM == w and bias.shape == (heads, t, rows + tn)
    tok = lambda r: pl.BlockSpec((None, r, w), lambda bi: (bi, 0, 0))
    return pl.pallas_call(
        _band_sample_kernel,
        out_shape=jax.ShapeDtypeStruct((b, t, w), F32),
        grid=(b,),
        in_specs=[tok(t), tok(tn), tok(tn), _cache_spec(cache_kt, layer),
                  _cache_spec(cache_vt, layer), pl.BlockSpec(bias.shape, lambda bi: (0, 0, 0))],
        out_specs=tok(t),
        compiler_params=pltpu.CompilerParams(dimension_semantics=("arbitrary",),
                                             vmem_limit_bytes=VMEM_LIMIT),
        name="band_attn_sample",
    )(q, k_new, v_new, cache_kt, cache_vt, bias)


def _sb_prompt_kernel(q_ref, kd_ref, vd_ref, kp_ref, vp_ref, tri_ref, causal_ref, o_ref, *, n_sub):
    rows = lambda u: slice(u * KEY_BLOCK, (u + 1) * KEY_BLOCK)
    head_lanes = _head_masks((KEY_BLOCK, LANES))
    streams, n_past = [], []
    for u in range(n_sub):
        q = q_ref[rows(u), :]
        for m in head_lanes:
            qh = jnp.where(m, q, jnp.zeros_like(q))

            def past(j, qh=qh):
                blk = pl.ds(pl.multiple_of(j * KEY_BLOCK, KEY_BLOCK), KEY_BLOCK)
                return (lambda: _nt(qh, kp_ref[blk, :])), (lambda w: _nn(w, vp_ref[blk, :]))

            diag = ((lambda qh=qh, u=u: _nt(qh, kd_ref[rows(u), :])),
                    (lambda w, u=u: _nn(w, vd_ref[rows(u), :])))
            streams.append((diag, past))
            n_past.append(pl.program_id(2) * n_sub + u)

    def store(pv, accumulate):
        for u in range(n_sub):
            val = jnp.where(head_lanes[0], pv[2 * u], pv[2 * u + 1])
            o_ref[rows(u), :] = o_ref[rows(u), :] + val if accumulate else val

    _sb_sweep(streams, n_past, tri_ref[...], causal_ref[...], store)


def _sb_attn_prompt(q, k, v, *, n_sub):
    b, t, w = q.shape
    rows = KEY_BLOCK * n_sub
    assert t % rows == 0 and w % LANES == 0
    tri, causal = _sb_constants(KEY_BLOCK)
    qmap = lambda bi, hp, i: (bi, i, hp)
    pmap = lambda bi, hp, i: (bi, 0, hp)
    const = lambda bi, hp, i: (0, 0)
    return pl.pallas_call(
        functools.partial(_sb_prompt_kernel, n_sub=n_sub),
        out_shape=jax.ShapeDtypeStruct((b, t, w), F32),
        grid=(b, w // LANES, t // rows),
        in_specs=[pl.BlockSpec((None, rows, LANES), qmap),
                  pl.BlockSpec((None, rows, LANES), qmap),
                  pl.BlockSpec((None, rows, LANES), qmap),
                  pl.BlockSpec((None, t, LANES), pmap),
                  pl.BlockSpec((None, t, LANES), pmap),
                  pl.BlockSpec((KEY_BLOCK, KEY_BLOCK), const),
                  pl.BlockSpec((KEY_BLOCK, KEY_BLOCK), const)],
        out_specs=pl.BlockSpec((None, rows, LANES), qmap),
        compiler_params=pltpu.CompilerParams(
            dimension_semantics=("parallel", "parallel", "arbitrary"),
            vmem_limit_bytes=VMEM_LIMIT),
        name="sb_attn",
    )(q, k, v, k, v, tri, causal)


def _band_prompt_kernel(q_ref, k_ref, v_ref, bias_ref, o_ref, *, bq, n_sub, kw, back):
    head_lanes = _head_masks((bq, LANES))

    def unit(u, h):
        first = (pl.program_id(2) * n_sub + u) * bq - back
        start = jnp.maximum(first, 0)
        win = pl.ds(pl.multiple_of(start, bq), kw)
        cols = pl.ds(pl.multiple_of(start - first, bq), kw)

        def scores():
            q = q_ref[u * bq:(u + 1) * bq, :]
            qh = jnp.where(head_lanes[h], q, jnp.zeros_like(q))
            return _nt(qh, k_ref[win, :]) + bias_ref[h, :, cols]

        return scores, lambda p: _nn(p, v_ref[win, :])

    outs = _run_skewed(_band_tasks([unit(u, h) for u in range(n_sub) for h in range(2)]))
    for u in range(n_sub):
        o_ref[u * bq:(u + 1) * bq, :] = jnp.where(head_lanes[0], outs[2 * u], outs[2 * u + 1])


def _band_attn_prompt(q, k, v, bias, *, bq, n_sub, kw, back):
    b, t, w = q.shape
    rows = bq * n_sub
    assert t % rows == 0 and w % LANES == 0 and back % bq == 0 and bias.shape[1:] == (bq, back + kw)
    return pl.pallas_call(
        functools.partial(_band_prompt_kernel, bq=bq, n_sub=n_sub, kw=kw, back=back),
        out_shape=jax.ShapeDtypeStruct((b, t, w), F32),
        grid=(b, w // LANES, t // rows),
        in_specs=[pl.BlockSpec((None, rows, LANES), lambda bi, hp, i: (bi, i, hp)),
                  pl.BlockSpec((None, t, LANES), lambda bi, hp, i: (bi, 0, hp)),
                  pl.BlockSpec((None, t, LANES), lambda bi, hp, i: (bi, 0, hp)),
                  pl.BlockSpec((2, bq, back + kw), lambda bi, hp, i: (hp, 0, 0))],
        out_specs=pl.BlockSpec((None, rows, LANES), lambda bi, hp, i: (bi, i, hp)),
        compiler_params=pltpu.CompilerParams(
            dimension_semantics=("parallel", "parallel", "arbitrary"),
            vmem_limit_bytes=VMEM_LIMIT),
        name="band_attn",
    )(q, k, v, bias)


def _band_bias(rel_bias, offset, valid):
    rows, cols = valid.shape
    w = rows + cols
    m = jnp.arange(w)
    rel = jnp.clip(offset - jnp.where(m < cols, m, m - w), -REL_CLIP, REL_CLIP) + REL_CLIP
    base = rel_bias.astype(F32)[:, rel] * LOG2E
    tiled = jnp.tile(base, (1, rows))[:, :rows * (w - 1)].reshape(-1, rows, w - 1)
    return jnp.where(valid[None], tiled[:, :, :cols], NEG_INF)


def _prompt_band_bias(rel_bias, bq, kw, back):
    qc = (back + jnp.arange(bq))[:, None] // CHUNK
    kc = jnp.arange(back + kw)[None, :] // CHUNK
    return _band_bias(rel_bias, back, (kc <= qc) & (kc >= qc - PAST_CHUNKS))


def _out_ffn_kernel(h_ref, osb_ref, obd_ref, gsb_ref, gbd_ref, wout_ref, gffn_ref, wup_ref, wdn_ref,
                    gfin_ref, y_ref, *, w_sb, ff_chunk, apply_final):
    ysb = _rms(osb_ref[...], gsb_ref[...]).astype(BF16)
    ybd = _rms(obd_ref[...], gbd_ref[...]).astype(BF16)
    h1 = h_ref[...] + _nn(ysb, wout_ref[:w_sb, :]) + _nn(ybd, wout_ref[w_sb:, :])
    xn = _rms(h1, gffn_ref[...]).astype(BF16)
    mlp = None
    for c in range(wup_ref.shape[1] // ff_chunk):
        u = _nn(xn, wup_ref[:, c * ff_chunk:(c + 1) * ff_chunk])
        a = jnp.square(jnp.maximum(u, 0.0)).astype(BF16)
        d = _nn(a, wdn_ref[c * ff_chunk:(c + 1) * ff_chunk, :])
        mlp = d if mlp is None else mlp + d
    h2 = h1 + mlp
    y_ref[...] = _rms(h2, gfin_ref[...]) if apply_final else h2


def _out_ffn(h2, osb, obd, g_sb, g_bd, w_out, g_ffn, w_up, w_dn, g_fin, *, apply_final):
    n, d = h2.shape
    w_sb, w_bd = osb.shape[1], obd.shape[1]
    tm = min(ROW_TILE, n)
    assert n % tm == 0
    row = lambda i: (i, 0)
    const = lambda i: (0, 0)
    resident = lambda a: pl.BlockSpec(a.shape, const, pipeline_mode=pl.Buffered(1))
    vec = lambda a: a.reshape(1, -1).astype(F32)
    args = (h2, osb, obd, vec(g_sb), vec(g_bd), w_out, vec(g_ffn), w_up, w_dn, vec(g_fin))
    in_specs = [pl.BlockSpec((tm, d), row), pl.BlockSpec((tm, w_sb), row),
                pl.BlockSpec((tm, w_bd), row)] + [resident(a) for a in args[3:]]
    return pl.pallas_call(
        functools.partial(_out_ffn_kernel, w_sb=w_sb, ff_chunk=min(1024, w_up.shape[1]),
                          apply_final=apply_final),
        out_shape=jax.ShapeDtypeStruct((n, d), F32),
        grid=(n // tm,),
        in_specs=in_specs,
        out_specs=pl.BlockSpec((tm, d), row),
        compiler_params=pltpu.CompilerParams(dimension_semantics=("arbitrary",),
                                             vmem_limit_bytes=VMEM_LIMIT),
        name="out_ffn",
    )(*args)


def kernel(x_prompt, x_sample, cache_sb_k, cache_sb_v, cache_band_k, cache_band_v, norm_mix_g, w_in,
           rel_bias, norm_sb_g, norm_band_g, w_out, norm_ffn_g, w_up, w_down, norm_final_g):
    depth = w_in.shape[0]
    bsz, seq, d_model = x_prompt.shape
    dbsz, dseq, _ = x_sample.shape
    past = cache_sb_k.shape[2]
    band_rows = cache_band_k.shape[2]
    n_sb, n_bd = cache_sb_k.shape[3], cache_band_k.shape[3]
    w_sb, w_bd = n_sb * HEAD_DIM, n_bd * HEAD_DIM
    keep_p = min(PAST_CHUNKS * CHUNK, seq)
    back = PAST_CHUNKS * CHUNK
    assert w_sb == w_bd and w_in.shape[2] == 6 * w_sb
    assert seq % KEY_BLOCK == 0 and seq >= back + KEY_BLOCK and keep_p == ROW_TILE
    assert band_rows == back and past >= band_rows
    sb_qscale = LOG2E / math.sqrt(HEAD_DIM)
    bd_qscale = LOG2E / math.sqrt(HEAD_DIM)
    time_minor = lambda c: jnp.transpose(c, (0, 1, 3, 4, 2))

    h_p, h_s = x_prompt, x_sample
    outs = [[] for _ in range(8)]
    for l in range(depth):
        w_in_l = w_in[l].astype(BF16)
        w_out_l, w_up_l, w_dn_l = (w[l].astype(BF16) for w in (w_out, w_up, w_down))
        last = l == depth - 1
        tail = dict(g_sb=norm_sb_g[l], g_bd=norm_band_g[l], w_out=w_out_l, g_ffn=norm_ffn_g[l],
                    w_up=w_up_l, w_dn=w_dn_l, g_fin=norm_final_g, apply_final=last)

        n = bsz * seq
        (qsb, ksb, vsb, qbd, kbd, vbd, ksb32, vsb32, kbd32, vbd32) = _norm_proj(
            h_p.reshape(n, d_model), norm_mix_g[l], w_in_l, tail_period=seq // ROW_TILE,
            sb_qscale=sb_qscale, bd_qscale=bd_qscale)
        r3 = lambda a: a.reshape(bsz, -1, a.shape[-1])
        qsb, ksb, vsb, qbd, kbd, vbd = map(r3, (qsb, ksb, vsb, qbd, kbd, vbd))
        o_sb = _sb_attn_prompt(qsb, ksb, vsb, n_sub=SB_SUB_BLOCKS)
        kw = back + KEY_BLOCK
        o_bd = _band_attn_prompt(qbd, kbd, vbd, _prompt_band_bias(rel_bias[l], KEY_BLOCK, kw, back),
                                 bq=KEY_BLOCK, n_sub=BAND_SUB_BLOCKS, kw=kw, back=back)
        h_p = _out_ffn(h_p.reshape(n, d_model), o_sb.reshape(n, w_sb), o_bd.reshape(n, w_bd),
                       **tail).reshape(bsz, seq, d_model)
        outs[0].append(ksb32.reshape(bsz, seq, n_sb, HEAD_DIM))
        outs[1].append(vsb32.reshape(bsz, seq, n_sb, HEAD_DIM))
        outs[2].append(kbd32.reshape(bsz, keep_p, n_bd, HEAD_DIM))
        outs[3].append(vbd32.reshape(bsz, keep_p, n_bd, HEAD_DIM))

        n = dbsz * dseq
        (qsb, ksb, vsb, qbd, kbd, vbd, ksb32, vsb32, kbd32, vbd32) = _norm_proj(
            h_s.reshape(n, d_model), norm_mix_g[l], w_in_l, tail_period=None,
            sb_qscale=sb_qscale, bd_qscale=bd_qscale)
        r3 = lambda a: a.reshape(dbsz, -1, a.shape[-1])
        qsb, ksb, vsb, qbd, kbd, vbd = map(r3, (qsb, ksb, vsb, qbd, kbd, vbd))
        o_sb = _sb_attn_sample(qsb, ksb, vsb, time_minor(cache_sb_k), time_minor(cache_sb_v), l)
        new_rows = -(-dseq // LANES) * LANES
        k_pos = past - band_rows + jnp.arange(band_rows + new_rows)
        valid = (k_pos >= 0) & (k_pos < past + dseq)
        bias = _band_bias(rel_bias[l], band_rows, jnp.broadcast_to(valid[None, :], (dseq, valid.size)))
        pad = lambda a: jnp.pad(a, ((0, 0), (0, new_rows - dseq), (0, 0)))
        o_bd = _band_attn_sample(qbd, pad(kbd), pad(vbd), time_minor(cache_band_k),
                                 time_minor(cache_band_v), bias, l)
        h_s = _out_ffn(h_s.reshape(n, d_model), o_sb.reshape(n, w_sb), o_bd.reshape(n, w_bd),
                       **tail).reshape(dbsz, dseq, d_model)
        outs[4].append(ksb32.reshape(dbsz, dseq, n_sb, HEAD_DIM))
        outs[5].append(vsb32.reshape(dbsz, dseq, n_sb, HEAD_DIM))
        outs[6].append(kbd32.reshape(dbsz, dseq, n_bd, HEAD_DIM))
        outs[7].append(vbd32.reshape(dbsz, dseq, n_bd, HEAD_DIM))

    return (h_p, h_s) + tuple(jnp.stack(o) for o in outs)
```

```python
import functools
import math

import jax
import jax.numpy as jnp
from jax import lax
from jax.experimental import pallas as pl
from jax.experimental.pallas import tpu as pltpu

F32 = jnp.float32
BF16 = jnp.bfloat16

HEAD_DIM = 64
CHUNK = 64
PAST_CHUNKS = 8
REL_CLIP = 2 * CHUNK
EPS = 1e-6
NEG_INF = -1e30
LOG2E = 1.4426950408889634
SB_CUTOFF = 150.0
SB_EXP_CLAMP = 64.0

LANES = 128
KEY_BLOCK = 256
ROW_TILE = 512
BAND_SUB_BLOCKS = 16
SB_SUB_BLOCKS = 8
VMEM_LIMIT = 56 * 1024 * 1024

_NT = (((1,), (1,)), ((), ()))


def _rms(x, g):
    return x * lax.rsqrt(jnp.mean(x * x, axis=-1, keepdims=True) + EPS) * g


def _head_masks(shape):
    lane = lax.broadcasted_iota(jnp.int32, shape, 1)
    return lane < HEAD_DIM, lane >= HEAD_DIM


def _nt(a, b):
    return lax.dot_general(a, b, _NT, preferred_element_type=F32)


def _nn(a, b):
    return jnp.dot(a, b, preferred_element_type=F32)


def _norm_proj_kernel(x_ref, g_ref, w_ref, qsb_ref, ksb_ref, vsb_ref, qbd_ref, kbd_ref, vbd_ref,
                      ksb32_ref, vsb32_ref, kbd32_ref, vbd32_ref, *, width, tail_period, sb_qscale,
                      bd_qscale):
    xn = _rms(x_ref[...], g_ref[...]).astype(BF16)

    def proj(c):
        return _nn(xn, w_ref[:, c * width:(c + 1) * width])

    def split_heads(p):
        return p.reshape(p.shape[0], -1, HEAD_DIM)

    qsb_ref[...] = (proj(0) * sb_qscale).astype(BF16)
    p = proj(1)
    ksb_ref[...] = p.astype(BF16)
    ksb32_ref[...] = split_heads(p)
    p = proj(2)
    vsb_ref[...] = p.astype(BF16)
    vsb32_ref[...] = split_heads(p)
    qbd_ref[...] = (proj(3) * bd_qscale).astype(BF16)
    pk = proj(4)
    kbd_ref[...] = pk.astype(BF16)
    pv = proj(5)
    vbd_ref[...] = pv.astype(BF16)

    def write_band32():
        kbd32_ref[...] = split_heads(pk)
        vbd32_ref[...] = split_heads(pv)

    if tail_period is None:
        write_band32()
    else:
        pl.when(pl.program_id(0) % tail_period == tail_period - 1)(write_band32)


def _norm_proj(x2, g, w_bf16, *, tail_period, sb_qscale, bd_qscale):
    n, d = x2.shape
    width = w_bf16.shape[1] // 6
    tm = min(ROW_TILE, n)
    assert n % tm == 0
    nt = n // tm
    row = lambda i: (i, 0)
    const = lambda i: (0, 0)
    heads = width // HEAD_DIM
    if tail_period is None:
        band_rows, band_map = n, (lambda i: (i, 0, 0))
    else:
        assert nt % tail_period == 0
        band_rows, band_map = (nt // tail_period) * tm, (lambda i: (i // tail_period, 0, 0))
    blk = pl.BlockSpec((tm, width), row)
    blk32 = lambda m: pl.BlockSpec((tm, heads, HEAD_DIM), m)
    out_shape = ([jax.ShapeDtypeStruct((n, width), BF16)] * 6
                 + [jax.ShapeDtypeStruct((n, heads, HEAD_DIM), F32)] * 2
                 + [jax.ShapeDtypeStruct((band_rows, heads, HEAD_DIM), F32)] * 2)
    return pl.pallas_call(
        functools.partial(_norm_proj_kernel, width=width, tail_period=tail_period,
                          sb_qscale=sb_qscale, bd_qscale=bd_qscale),
        out_shape=out_shape,
        grid=(nt,),
        in_specs=[pl.BlockSpec((tm, d), row), pl.BlockSpec((1, d), const),
                  pl.BlockSpec(w_bf16.shape, const)],
        out_specs=[blk] * 6 + [blk32(lambda i: (i, 0, 0))] * 2 + [blk32(band_map)] * 2,
        compiler_params=pltpu.CompilerParams(dimension_semantics=("arbitrary",),
                                             vmem_limit_bytes=VMEM_LIMIT),
        name="norm_proj",
    )(x2, g.reshape(1, d), w_bf16)


def _sb_softplus2(z):
    return jnp.maximum(z, jnp.log2(1.0 + jnp.exp2(jnp.minimum(z, SB_EXP_CLAMP))))


def _run_skewed(tasks):
    depth = max(len(t) for t in tasks)
    vals = [None] * len(tasks)
    for i in range(len(tasks) + depth - 1):
        for k in range(depth):
            j = i - k
            if 0 <= j < len(tasks) and k < len(tasks[j]):
                vals[j] = tasks[j][k](vals[j])
    return vals


def _below_diagonal(fn, *tiles):
    h, w = tiles[0].shape[0] // 2, tiles[0].shape[1] // 2
    top = fn(*[t[:h, :w] for t in tiles])
    bottom = fn(*[t[h:, :] for t in tiles])
    return jnp.concatenate([jnp.concatenate([top, jnp.zeros_like(top)], axis=1), bottom], axis=0)


def _sb_tasks(chains, tri, swept):
    def task(score_fn, pv_fn, bias, carry_fn):
        square = bias is not None and bias.shape[0] == bias.shape[1]
        on = _below_diagonal if square else (lambda fn, *tiles: fn(*tiles))

        def scores(_):
            return score_fn() if bias is None else on(jnp.add, score_fn(), bias)

        def suffix(z):
            sp = on(_sb_softplus2, z)
            swept.append(jnp.sum(sp, axis=-1, keepdims=True))
            return on(jnp.subtract, z, sp), _nn(sp.astype(BF16), tri)

        def weigh(stage):
            carry = carry_fn(swept)
            if carry is None:
                w = on(lambda ls, excl: jnp.exp2(ls - excl).astype(BF16), *stage)
            else:
                w = jnp.exp2(stage[0] - stage[1] - carry).astype(BF16)
            return pv_fn(w)

        return [scores, suffix, weigh]

    return [task(*c) for c in chains]


def _sb_sweep(streams, n_past, tri, causal, store):
    ids = range(len(streams))
    big = lambda ok: jnp.where(ok, 0.0, -NEG_INF).astype(F32)

    chains = []
    no_past = [big(n > 0) for n in n_past]
    for s in ids:
        diag, past = streams[s]
        chains.append((*diag, causal, lambda swept: None))
        chains.append((*past(jnp.maximum(n_past[s] - 1, 0)), None,
                       lambda swept, s=s: swept[2 * s] + no_past[s]))
    swept = []
    pv = _run_skewed(_sb_tasks(chains, tri, swept))
    store([pv[2 * s] + pv[2 * s + 1] for s in ids], False)
    carries = [swept[2 * s] + no_past[s] + swept[2 * s + 1] for s in ids]

    def least(cs):
        return jnp.min(functools.reduce(jnp.minimum, cs))

    most_past = functools.reduce(jnp.maximum, n_past)

    def cond(state):
        return jnp.logical_and(state[0] < most_past - 1, state[1] < SB_CUTOFF)

    def body(state):
        t, cs = state[0], state[2:]
        chains, start = [], []
        for s in ids:
            j = n_past[s] - 2 - t
            start.append(cs[s] + big(j >= 0))
            chains.append((*streams[s][1](jnp.maximum(j, 0)), None, lambda swept, c=start[-1]: c))
        swept = []
        store(_run_skewed(_sb_tasks(chains, tri, swept)), True)
        new = [c + w for c, w in zip(start, swept)]
        return (t + 1, least(new), *new)

    lax.while_loop(cond, body, (0, least(carries), *carries))


def _sb_sample_kernel(q_ref, kn_ref, vn_ref, ck_ref, cv_ref, tri_ref, causal_ref, o_ref, *, n_past):
    heads = ck_ref.shape[0]
    streams = []
    for h in range(heads):
        hl = slice(h * HEAD_DIM, (h + 1) * HEAD_DIM)
        qh = q_ref[:, hl]

        def past(j, h=h, qh=qh):
            blk = pl.ds(pl.multiple_of(j * KEY_BLOCK, KEY_BLOCK), KEY_BLOCK)
            return ((lambda: _nn(qh, ck_ref[h, :, blk].astype(BF16))),
                    (lambda w: _nt(w, cv_ref[h, :, blk].astype(BF16))))

        diag = ((lambda qh=qh, hl=hl: _nt(qh, kn_ref[:, hl])), (lambda w, hl=hl: _nn(w, vn_ref[:, hl])))
        streams.append((diag, past))

    def store(pv, accumulate):
        for h in range(heads):
            hl = slice(h * HEAD_DIM, (h + 1) * HEAD_DIM)
            o_ref[:, hl] = o_ref[:, hl] + pv[h] if accumulate else pv[h]

    _sb_sweep(streams, [n_past] * heads, tri_ref[...], causal_ref[...], store)


def _sb_constants(bq):
    row = lax.broadcasted_iota(jnp.int32, (KEY_BLOCK, KEY_BLOCK), 0)
    col = lax.broadcasted_iota(jnp.int32, (KEY_BLOCK, KEY_BLOCK), 1)
    tri = (row > col).astype(BF16)
    causal = jnp.where(col < row, 0.0, NEG_INF).astype(F32)[:bq]
    return tri, causal


def _cache_spec(cache_t, layer):
    return pl.BlockSpec((None, None) + cache_t.shape[2:], lambda bi: (layer, bi, 0, 0, 0))


def _sb_attn_sample(q, k_new, v_new, cache_kt, cache_vt, layer):
    b, t, w = q.shape
    heads, past = cache_kt.shape[2], cache_kt.shape[4]
    assert t <= KEY_BLOCK and past % KEY_BLOCK == 0 and heads * HEAD_DIM == w
    tri, causal = _sb_constants(t)
    pad = lambda a: jnp.pad(a, ((0, 0), (0, KEY_BLOCK - t), (0, 0)))
    tok = lambda rows: pl.BlockSpec((None, rows, w), lambda bi: (bi, 0, 0))
    const = lambda bi: (0, 0)
    return pl.pallas_call(
        functools.partial(_sb_sample_kernel, n_past=past // KEY_BLOCK),
        out_shape=jax.ShapeDtypeStruct((b, t, w), F32),
        grid=(b,),
        in_specs=[tok(t), tok(KEY_BLOCK), tok(KEY_BLOCK), _cache_spec(cache_kt, layer),
                  _cache_spec(cache_vt, layer), pl.BlockSpec((KEY_BLOCK, KEY_BLOCK), const),
                  pl.BlockSpec((t, KEY_BLOCK), const)],
        out_specs=tok(t),
        compiler_params=pltpu.CompilerParams(dimension_semantics=("arbitrary",),
                                             vmem_limit_bytes=VMEM_LIMIT),
        name="sb_attn_sample",
    )(q, pad(k_new), pad(v_new), cache_kt, cache_vt, tri, causal)


def _band_tasks(units, key_axis=-1):
    def task(score_fn, pv_fn):
        def attend(s):
            p = jnp.exp2(s - jnp.max(s, axis=key_axis, keepdims=True))
            den = jnp.sum(p, axis=key_axis, keepdims=True)
            return pv_fn(p.astype(BF16)) / den

        return [lambda _: score_fn(), lambda s: s, attend]

    return [task(*u) for u in units]


def _band_sample_kernel(q_ref, kn_ref, vn_ref, ck_ref, cv_ref, bias_ref, o_ref):
    heads, _, rows = ck_ref.shape

    def unit(h):
        hl = slice(h * HEAD_DIM, (h + 1) * HEAD_DIM)

        def scores():
            qh = q_ref[:, hl]
            s = jnp.concatenate([_nn(qh, ck_ref[h].astype(BF16)), _nt(qh, kn_ref[:, hl])], axis=1)
            return s + bias_ref[h]

        def pv(p):
            return _nt(p[:, :rows], cv_ref[h].astype(BF16)) + _nn(p[:, rows:], vn_ref[:, hl])

        return scores, pv

    outs = _run_skewed(_band_tasks([unit(h) for h in range(heads)]))
    for h in range(heads):
        o_ref[:, h * HEAD_DIM:(h + 1) * HEAD_DIM] = outs[h]


def _band_attn_sample(q, k_new, v_new, cache_kt, cache_vt, bias, layer):
    b, t, w = q.shape
    tn = k_new.shape[1]
    heads, rows = cache_kt.shape[2], cache_kt.shape[4]
    assert heads * HEAD_DIM == w and bias.shape == (heads, t, rows + tn)
    tok = lambda r: pl.BlockSpec((None, r, w), lambda bi: (bi, 0, 0))
    return pl.pallas_call(
        _band_sample_kernel,
        out_shape=jax.ShapeDtypeStruct((b, t, w), F32),
        grid=(b,),
        in_specs=[tok(t), tok(tn), tok(tn), _cache_spec(cache_kt, layer),
                  _cache_spec(cache_vt, layer), pl.BlockSpec(bias.shape, lambda bi: (0, 0, 0))],
        out_specs=tok(t),
        compiler_params=pltpu.CompilerParams(dimension_semantics=("arbitrary",),
                                             vmem_limit_bytes=VMEM_LIMIT),
        name="band_attn_sample",
    )(q, k_new, v_new, cache_kt, cache_vt, bias)


def _sb_prompt_kernel(q_ref, kd_ref, vd_ref, kp_ref, vp_ref, tri_ref, causal_ref, o_ref, *, n_sub):
    rows = lambda u: slice(u * KEY_BLOCK, (u + 1) * KEY_BLOCK)
    head_lanes = _head_masks((KEY_BLOCK, LANES))
    streams, n_past = [], []
    for u in range(n_sub):
        q = q_ref[rows(u), :]
        for m in head_lanes:
            qh = jnp.where(m, q, jnp.zeros_like(q))

            def past(j, qh=qh):
                blk = pl.ds(pl.multiple_of(j * KEY_BLOCK, KEY_BLOCK), KEY_BLOCK)
                return (lambda: _nt(qh, kp_ref[blk, :])), (lambda w: _nn(w, vp_ref[blk, :]))

            diag = ((lambda qh=qh, u=u: _nt(qh, kd_ref[rows(u), :])),
                    (lambda w, u=u: _nn(w, vd_ref[rows(u), :])))
            streams.append((diag, past))
            n_past.append(pl.program_id(2) * n_sub + u)

    def store(pv, accumulate):
        for u in range(n_sub):
            val = jnp.where(head_lanes[0], pv[2 * u], pv[2 * u + 1])
            o_ref[rows(u), :] = o_ref[rows(u), :] + val if accumulate else val

    _sb_sweep(streams, n_past, tri_ref[...], causal_ref[...], store)


def _sb_attn_prompt(q, k, v, *, n_sub):
    b, t, w = q.shape
    rows = KEY_BLOCK * n_sub
    assert t % rows == 0 and w % LANES == 0
    tri, causal = _sb_constants(KEY_BLOCK)
    qmap = lambda bi, hp, i: (bi, i, hp)
    pmap = lambda bi, hp, i: (bi, 0, hp)
    const = lambda bi, hp, i: (0, 0)
    return pl.pallas_call(
        functools.partial(_sb_prompt_kernel, n_sub=n_sub),
        out_shape=jax.ShapeDtypeStruct((b, t, w), F32),
        grid=(b, w // LANES, t // rows),
        in_specs=[pl.BlockSpec((None, rows, LANES), qmap),
                  pl.BlockSpec((None, rows, LANES), qmap),
                  pl.BlockSpec((None, rows, LANES), qmap),
                  pl.BlockSpec((None, t, LANES), pmap),
                  pl.BlockSpec((None, t, LANES), pmap),
                  pl.BlockSpec((KEY_BLOCK, KEY_BLOCK), const),
                  pl.BlockSpec((KEY_BLOCK, KEY_BLOCK), const)],
        out_specs=pl.BlockSpec((None, rows, LANES), qmap),
        compiler_params=pltpu.CompilerParams(
            dimension_semantics=("parallel", "parallel", "arbitrary"),
            vmem_limit_bytes=VMEM_LIMIT),
        name="sb_attn",
    )(q, k, v, k, v, tri, causal)


def _band_prompt_kernel(q_ref, k_ref, v_ref, bias_ref, o_ref, *, bq, n_sub, kw, back):
    head_lanes = _head_masks((bq, LANES))

    def unit(u, h):
        first = (pl.program_id(2) * n_sub + u) * bq - back
        start = jnp.maximum(first, 0)
        win = pl.ds(pl.multiple_of(start, bq), kw)
        cols = pl.ds(pl.multiple_of(start - first, bq), kw)

        def scores():
            q = q_ref[u * bq:(u + 1) * bq, :]
            qh = jnp.where(head_lanes[h], q, jnp.zeros_like(q))
            return _nt(k_ref[win, :], qh) + bias_ref[h, cols, :]

        def pv(p):
            return lax.dot_general(v_ref[win, :], p, (((0,), (0,)), ((), ())),
                                   preferred_element_type=F32)

        return scores, pv

    outs = _run_skewed(_band_tasks([unit(u, h) for u in range(n_sub) for h in range(2)], key_axis=0))
    for u in range(n_sub):
        out_t = jnp.concatenate([outs[2 * u][:HEAD_DIM], outs[2 * u + 1][HEAD_DIM:]], axis=0)
        o_ref[u * bq:(u + 1) * bq, :] = out_t.T


def _band_attn_prompt(q, k, v, bias, *, bq, n_sub, kw, back):
    b, t, w = q.shape
    rows = bq * n_sub
    assert t % rows == 0 and w % LANES == 0 and back % bq == 0 and bias.shape[1:] == (bq, back + kw)
    bias_t = jnp.swapaxes(bias, 1, 2)
    return pl.pallas_call(
        functools.partial(_band_prompt_kernel, bq=bq, n_sub=n_sub, kw=kw, back=back),
        out_shape=jax.ShapeDtypeStruct((b, t, w), F32),
        grid=(b, w // LANES, t // rows),
        in_specs=[pl.BlockSpec((None, rows, LANES), lambda bi, hp, i: (bi, i, hp)),
                  pl.BlockSpec((None, t, LANES), lambda bi, hp, i: (bi, 0, hp)),
                  pl.BlockSpec((None, t, LANES), lambda bi, hp, i: (bi, 0, hp)),
                  pl.BlockSpec((2, back + kw, bq), lambda bi, hp, i: (hp, 0, 0))],
        out_specs=pl.BlockSpec((None, rows, LANES), lambda bi, hp, i: (bi, i, hp)),
        compiler_params=pltpu.CompilerParams(
            dimension_semantics=("parallel", "parallel", "arbitrary"),
            vmem_limit_bytes=VMEM_LIMIT),
        name="band_attn",
    )(q, k, v, bias_t)


def _band_bias(rel_bias, offset, valid):
    rows, cols = valid.shape
    w = rows + cols
    m = jnp.arange(w)
    rel = jnp.clip(offset - jnp.where(m < cols, m, m - w), -REL_CLIP, REL_CLIP) + REL_CLIP
    base = rel_bias.astype(F32)[:, rel] * LOG2E
    tiled = jnp.tile(base, (1, rows))[:, :rows * (w - 1)].reshape(-1, rows, w - 1)
    return jnp.where(valid[None], tiled[:, :, :cols], NEG_INF)


def _prompt_band_bias(rel_bias, bq, kw, back):
    qc = (back + jnp.arange(bq))[:, None] // CHUNK
    kc = jnp.arange(back + kw)[None, :] // CHUNK
    return _band_bias(rel_bias, back, (kc <= qc) & (kc >= qc - PAST_CHUNKS))


def _out_ffn_kernel(h_ref, osb_ref, obd_ref, gsb_ref, gbd_ref, wout_ref, gffn_ref, wup_ref, wdn_ref,
                    gfin_ref, y_ref, *, w_sb, ff_chunk, apply_final):
    ysb = _rms(osb_ref[...], gsb_ref[...]).astype(BF16)
    ybd = _rms(obd_ref[...], gbd_ref[...]).astype(BF16)
    h1 = h_ref[...] + _nn(ysb, wout_ref[:w_sb, :]) + _nn(ybd, wout_ref[w_sb:, :])
    xn = _rms(h1, gffn_ref[...]).astype(BF16)
    mlp = None
    for c in range(wup_ref.shape[1] // ff_chunk):
        u = _nn(xn, wup_ref[:, c * ff_chunk:(c + 1) * ff_chunk])
        a = jnp.square(jnp.maximum(u, 0.0)).astype(BF16)
        d = _nn(a, wdn_ref[c * ff_chunk:(c + 1) * ff_chunk, :])
        mlp = d if mlp is None else mlp + d
    h2 = h1 + mlp
    y_ref[...] = _rms(h2, gfin_ref[...]) if apply_final else h2


def _out_ffn(h2, osb, obd, g_sb, g_bd, w_out, g_ffn, w_up, w_dn, g_fin, *, apply_final):
    n, d = h2.shape
    w_sb, w_bd = osb.shape[1], obd.shape[1]
    tm = min(ROW_TILE, n)
    assert n % tm == 0
    row = lambda i: (i, 0)
    const = lambda i: (0, 0)
    resident = lambda a: pl.BlockSpec(a.shape, const, pipeline_mode=pl.Buffered(1))
    vec = lambda a: a.reshape(1, -1).astype(F32)
    args = (h2, osb, obd, vec(g_sb), vec(g_bd), w_out, vec(g_ffn), w_up, w_dn, vec(g_fin))
    in_specs = [pl.BlockSpec((tm, d), row), pl.BlockSpec((tm, w_sb), row),
                pl.BlockSpec((tm, w_bd), row)] + [resident(a) for a in args[3:]]
    return pl.pallas_call(
        functools.partial(_out_ffn_kernel, w_sb=w_sb, ff_chunk=min(1024, w_up.shape[1]),
                          apply_final=apply_final),
        out_shape=jax.ShapeDtypeStruct((n, d), F32),
        grid=(n // tm,),
        in_specs=in_specs,
        out_specs=pl.BlockSpec((tm, d), row),
        compiler_params=pltpu.CompilerParams(dimension_semantics=("arbitrary",),
                                             vmem_limit_bytes=VMEM_LIMIT),
        name="out_ffn",
    )(*args)


def kernel(x_prompt, x_sample, cache_sb_k, cache_sb_v, cache_band_k, cache_band_v, norm_mix_g, w_in,
           rel_bias, norm_sb_g, norm_band_g, w_out, norm_ffn_g, w_up, w_down, norm_final_g):
    depth = w_in.shape[0]
    bsz, seq, d_model = x_prompt.shape
    dbsz, dseq, _ = x_sample.shape
    past = cache_sb_k.shape[2]
    band_rows = cache_band_k.shape[2]
    n_sb, n_bd = cache_sb_k.shape[3], cache_band_k.shape[3]
    w_sb, w_bd = n_sb * HEAD_DIM, n_bd * HEAD_DIM
    keep_p = min(PAST_CHUNKS * CHUNK, seq)
    back = PAST_CHUNKS * CHUNK
    assert w_sb == w_bd and w_in.shape[2] == 6 * w_sb
    assert seq % KEY_BLOCK == 0 and seq >= back + KEY_BLOCK and keep_p == ROW_TILE
    assert band_rows == back and past >= band_rows
    sb_qscale = LOG2E / math.sqrt(HEAD_DIM)
    bd_qscale = LOG2E / math.sqrt(HEAD_DIM)
    time_minor = lambda c: jnp.transpose(c, (0, 1, 3, 4, 2))

    h_p, h_s = x_prompt, x_sample
    outs = [[] for _ in range(8)]
    for l in range(depth):
        w_in_l = w_in[l].astype(BF16)
        w_out_l, w_up_l, w_dn_l = (w[l].astype(BF16) for w in (w_out, w_up, w_down))
        last = l == depth - 1
        tail = dict(g_sb=norm_sb_g[l], g_bd=norm_band_g[l], w_out=w_out_l, g_ffn=norm_ffn_g[l],
                    w_up=w_up_l, w_dn=w_dn_l, g_fin=norm_final_g, apply_final=last)

        n = bsz * seq
        (qsb, ksb, vsb, qbd, kbd, vbd, ksb32, vsb32, kbd32, vbd32) = _norm_proj(
            h_p.reshape(n, d_model), norm_mix_g[l], w_in_l, tail_period=seq // ROW_TILE,
            sb_qscale=sb_qscale, bd_qscale=bd_qscale)
        r3 = lambda a: a.reshape(bsz, -1, a.shape[-1])
        qsb, ksb, vsb, qbd, kbd, vbd = map(r3, (qsb, ksb, vsb, qbd, kbd, vbd))
        o_sb = _sb_attn_prompt(qsb, ksb, vsb, n_sub=SB_SUB_BLOCKS)
        kw = back + KEY_BLOCK
        o_bd = _band_attn_prompt(qbd, kbd, vbd, _prompt_band_bias(rel_bias[l], KEY_BLOCK, kw, back),
                                 bq=KEY_BLOCK, n_sub=BAND_SUB_BLOCKS, kw=kw, back=back)
        h_p = _out_ffn(h_p.reshape(n, d_model), o_sb.reshape(n, w_sb), o_bd.reshape(n, w_bd),
                       **tail).reshape(bsz, seq, d_model)
        outs[0].append(ksb32.reshape(bsz, seq, n_sb, HEAD_DIM))
        outs[1].append(vsb32.reshape(bsz, seq, n_sb, HEAD_DIM))
        outs[2].append(kbd32.reshape(bsz, keep_p, n_bd, HEAD_DIM))
        outs[3].append(vbd32.reshape(bsz, keep_p, n_bd, HEAD_DIM))

        n = dbsz * dseq
        (qsb, ksb, vsb, qbd, kbd, vbd, ksb32, vsb32, kbd32, vbd32) = _norm_proj(
            h_s.reshape(n, d_model), norm_mix_g[l], w_in_l, tail_period=None,
            sb_qscale=sb_qscale, bd_qscale=bd_qscale)
        r3 = lambda a: a.reshape(dbsz, -1, a.shape[-1])
        qsb, ksb, vsb, qbd, kbd, vbd = map(r3, (qsb, ksb, vsb, qbd, kbd, vbd))
        o_sb = _sb_attn_sample(qsb, ksb, vsb, time_minor(cache_sb_k), time_minor(cache_sb_v), l)
        new_rows = -(-dseq // LANES) * LANES
        k_pos = past - band_rows + jnp.arange(band_rows + new_rows)
        valid = (k_pos >= 0) & (k_pos < past + dseq)
        bias = _band_bias(rel_bias[l], band_rows, jnp.broadcast_to(valid[None, :], (dseq, valid.size)))
        pad = lambda a: jnp.pad(a, ((0, 0), (0, new_rows - dseq), (0, 0)))
        o_bd = _band_attn_sample(qbd, pad(kbd), pad(vbd), time_minor(cache_band_k),
                                 time_minor(cache_band_v), bias, l)
        h_s = _out_ffn(h_s.reshape(n, d_model), o_sb.reshape(n, w_sb), o_bd.reshape(n, w_bd),
                       **tail).reshape(dbsz, dseq, d_model)
        outs[4].append(ksb32.reshape(dbsz, dseq, n_sb, HEAD_DIM))
        outs[5].append(vsb32.reshape(dbsz, dseq, n_sb, HEAD_DIM))
        outs[6].append(kbd32.reshape(dbsz, dseq, n_bd, HEAD_DIM))
        outs[7].append(vbd32.reshape(dbsz, dseq, n_bd, HEAD_DIM))

    return (h_p, h_s) + tuple(jnp.stack(o) for o in outs)
```
